```python
import math
import jax, jax.numpy as jnp
from jax import lax
import numpy as np

D_MODEL = 2048
BATCH = 1
SEQ = 16384
DEPTH = 2
DEC_BATCH = 16
DEC_SEQ = 2048
PAST_LEN = 128

N_EVEN = (DEPTH + 1) // 2
N_ODD = DEPTH // 2
D_FF = 4 * D_MODEL
EPS = 1e-6
ROPE_THETA = 500000.0
Q_BLOCK = 128

SSD_HEADS = 16
SSD_HEAD_DIM = 64
D_SSM = SSD_HEADS * SSD_HEAD_DIM
SSD_GROUPS = 2
SSD_STATE = 128
HEADS_PER_GROUP = SSD_HEADS // SSD_GROUPS
CONV_K = 5
CONV_DIM = D_SSM + 2 * SSD_GROUPS * SSD_STATE
SSD_CHUNK = 128

MLA_HEADS = 8
QK_NOPE = 128
QK_ROPE = 64
V_HEAD = 128
Q_LORA = 512
KV_LORA = 512
D_MLA = MLA_HEADS * V_HEAD

D_IN_EVEN = D_SSM + CONV_DIM + 2 * SSD_HEADS + Q_LORA + KV_LORA + QK_ROPE
D_MIX_EVEN = D_SSM + D_MLA

DIFF_HEADS = 8
DIFF_HEAD_DIM = 128
DIFF_ROT = DIFF_HEAD_DIM // 4
D_DIFF = DIFF_HEADS * 2 * DIFF_HEAD_DIM

kernel_name = 'hybrid_bidir_ssd_mla_diffattn_encoder'


def _rms_norm(x, w):
    xf = x.astype(jnp.float32)
    y = xf * lax.rsqrt(jnp.mean(xf * xf, axis=-1, keepdims=True) + EPS)
    return (y * w.astype(jnp.float32)).astype(x.dtype)


def _rope(x, rot_dim):
    L = x.shape[1]
    half = rot_dim // 2
    inv_freq = ROPE_THETA ** (-jnp.arange(half, dtype=jnp.float32) / half)
    ang = jnp.arange(L, dtype=jnp.float32)[:, None] * inv_freq[None, :]
    shape = (L,) + (1,) * (x.ndim - 3) + (half,)
    cos = jnp.cos(ang).reshape(shape)
    sin = jnp.sin(ang).reshape(shape)
    xr = x[..., :rot_dim].astype(jnp.float32)
    x1, x2 = xr[..., :half], xr[..., half:]
    rot = jnp.concatenate([x1 * cos - x2 * sin, x1 * sin + x2 * cos], axis=-1).astype(x.dtype)
    return jnp.concatenate([rot, x[..., rot_dim:]], axis=-1)


def _map_query_blocks(fn, qs):
    b, L = qs[0].shape[:2]
    nb = L // Q_BLOCK
    blocks = tuple(jnp.moveaxis(q.reshape((b, nb, Q_BLOCK) + q.shape[2:]), 1, 0) for q in qs)
    out = jnp.moveaxis(lax.map(fn, blocks), 0, 1)
    return out.reshape((b, L) + out.shape[3:])


def _dwconv_centred(x, w, bias):
    c = x.shape[-1]
    y = lax.conv_general_dilated(x, w.astype(x.dtype)[:, None, :], window_strides=(1,),
                                 padding=[(CONV_K // 2, CONV_K // 2)],
                                 dimension_numbers=('NWC', 'WIO', 'NWC'), feature_group_count=c)
    return y + bias.astype(x.dtype)


def _ssd_chunked(x, dt, a, bm, cm):
    b, L = x.shape[:2]
    nc = L // SSD_CHUNK
    xc = (x * dt[..., None]).reshape(b, nc, SSD_CHUNK, SSD_GROUPS, HEADS_PER_GROUP, SSD_HEAD_DIM)
    da = (dt * a).reshape(b, nc, SSD_CHUNK, SSD_GROUPS, HEADS_PER_GROUP)
    bc = bm.reshape(b, nc, SSD_CHUNK, SSD_GROUPS, SSD_STATE)
    cc = cm.reshape(b, nc, SSD_CHUNK, SSD_GROUPS, SSD_STATE)
    cum = jnp.cumsum(da, axis=2)
    mask = jnp.tril(jnp.ones((SSD_CHUNK, SSD_CHUNK), dtype=bool))[:, :, None, None]
    decay = jnp.exp(jnp.where(mask, cum[:, :, :, None] - cum[:, :, None, :], -jnp.inf))
    cb = jnp.einsum('bclgn,bcsgn->bclsg', cc, bc)
    y_diag = jnp.einsum('bclsge,bcsgep->bclgep', cb[..., None] * decay, xc)
    decay_to_end = jnp.exp(cum[:, :, -1:] - cum)
    states = jnp.einsum('bclgn,bclgep->bcgepn', bc, xc * decay_to_end[..., None])
    chunk_decay = jnp.exp(cum[:, :, -1])

    def step(hs, inp):
        st, dec = inp
        return hs * dec[..., None, None] + st, hs

    h0 = jnp.zeros((b, SSD_GROUPS, HEADS_PER_GROUP, SSD_HEAD_DIM, SSD_STATE), jnp.float32)
    _, h_prev = lax.scan(step, h0, (jnp.moveaxis(states, 1, 0), jnp.moveaxis(chunk_decay, 1, 0)))
    h_prev = jnp.moveaxis(h_prev, 0, 1)
    y_off = jnp.einsum('bclgn,bcgepn->bclgep', cc, h_prev) * jnp.exp(cum)[..., None]
    return (y_diag + y_off).reshape(b, L, SSD_HEADS, SSD_HEAD_DIM)


def _ssd_mla_mixer(h, p, i):
    b, L, _ = h.shape
    proj = h @ p['w_in_even'][i]
    o1 = D_SSM
    o2 = o1 + CONV_DIM
    o3 = o2 + 2 * SSD_HEADS
    o4 = o3 + Q_LORA
    o5 = o4 + KV_LORA
    z, xbc, dt_raw, c_q, c_kv, k_pe = jnp.split(proj, [o1, o2, o3, o4, o5], axis=-1)

    xbc = jax.nn.silu(_dwconv_centred(xbc, p['conv_w'][i], p['conv_b'][i]))
    xs, bm, cm = jnp.split(xbc, [D_SSM, D_SSM + SSD_GROUPS * SSD_STATE], axis=-1)
    xs = xs.reshape(b, L, SSD_HEADS, SSD_HEAD_DIM).astype(jnp.float32)
    bm = bm.reshape(b, L, SSD_GROUPS, SSD_STATE).astype(jnp.float32)
    cm = cm.reshape(b, L, SSD_GROUPS, SSD_STATE).astype(jnp.float32)
    dt_raw = dt_raw.astype(jnp.float32)
    dt_bias = p['dt_bias'][i].astype(jnp.float32)
    a_log = p['a_log'][i].astype(jnp.float32)
    dt_f = jax.nn.softplus(dt_raw[..., :SSD_HEADS] + dt_bias[0])
    dt_b = jax.nn.softplus(dt_raw[..., SSD_HEADS:] + dt_bias[1])
    flip = lambda t: jnp.flip(t, axis=1)
    y_f = _ssd_chunked(xs, dt_f, -jnp.exp(a_log[0]), bm, cm)
    y_b = flip(_ssd_chunked(flip(xs), flip(dt_b), -jnp.exp(a_log[1]), flip(bm), flip(cm)))
    y = y_f + y_b + xs * p['d_skip'][i].astype(jnp.float32)[:, None]
    y = y.reshape(b, L, D_SSM) * jax.nn.silu(z.astype(jnp.float32))
    y = _rms_norm(y.reshape(b, L, SSD_GROUPS, D_SSM // SSD_GROUPS),
                  p['ssm_norm_w'][i].reshape(SSD_GROUPS, D_SSM // SSD_GROUPS)).reshape(b, L, D_SSM)

    q = (_rms_norm(c_q, p['q_norm_w'][i]) @ p['w_q_up'][i]).reshape(b, L, MLA_HEADS, QK_NOPE + QK_ROPE)
    q_nope = q[..., :QK_NOPE]
    q_pe = _rope(q[..., QK_NOPE:], QK_ROPE)
    kv = (_rms_norm(c_kv, p['kv_norm_w'][i]) @ p['w_kv_up'][i]).reshape(b, L, MLA_HEADS, QK_NOPE + V_HEAD)
    k_nope = kv[..., :QK_NOPE]
    v = kv[..., QK_NOPE:]
    k_pe = _rope(k_pe, QK_ROPE)
    scale = 1.0 / math.sqrt(QK_NOPE + QK_ROPE)

    def block(qb):
        qn, qp = qb
        s = (jnp.einsum('bqhd,bkhd->bhqk', qn, k_nope) +
             jnp.einsum('bqhr,bkr->bhqk', qp, k_pe)).astype(jnp.float32) * scale
        pr = jax.nn.softmax(s, axis=-1).astype(v.dtype)
        return jnp.einsum('bhqk,bkhd->bqhd', pr, v)

    o = _map_query_blocks(block, (q_nope, q_pe)).reshape(b, L, D_MLA)
    mix = jnp.concatenate([y.astype(h.dtype), o.astype(h.dtype)], axis=-1)
    return mix @ p['w_out_even'][i]


def _diff_mixer(h, p, j, layer_idx):
    b, L, _ = h.shape
    q, k, v = jnp.split(h @ p['w_in_odd'][j], 3, axis=-1)
    q = _rope(q.reshape(b, L, DIFF_HEADS, 2, DIFF_HEAD_DIM), DIFF_ROT)
    k = _rope(k.reshape(b, L, DIFF_HEADS, 2, DIFF_HEAD_DIM), DIFF_ROT)
    v = v.reshape(b, L, DIFF_HEADS, 2 * DIFF_HEAD_DIM)
    lam_init = 0.8 - 0.6 * math.exp(-0.3 * layer_idx)
    lp = p['diff_lambda'][j].astype(jnp.float32)
    lam = jnp.exp(jnp.sum(lp[0] * lp[1])) - jnp.exp(jnp.sum(lp[2] * lp[3])) + lam_init
    k1 = k[..., 0, :]
    k2 = k[..., 1, :]
    scale = 1.0 / math.sqrt(DIFF_HEAD_DIM)

    def block(qb):
        q1, q2 = qb
        s1 = jnp.einsum('bqhd,bkhd->bhqk', q1, k1).astype(jnp.float32) * scale
        s2 = jnp.einsum('bqhd,bkhd->bhqk', q2, k2).astype(jnp.float32) * scale
        att = jax.nn.softmax(s1, axis=-1) - lam * jax.nn.softmax(s2, axis=-1)
        return jnp.einsum('bhqk,bkhe->bqhe', att.astype(v.dtype), v)

    o = _map_query_blocks(block, (q[..., 0, :], q[..., 1, :]))
    o = _rms_norm(o, p['subln_w'][j]) * (1.0 - lam_init)
    return o.reshape(b, L, D_DIFF) @ p['w_out_odd'][j]


def _trunk(x, p):
    h = x
    for i in range(DEPTH):
        u = _rms_norm(h, p['norm_pre_mix'][i])
        if i % 2 == 0:
            m = _ssd_mla_mixer(u, p, i // 2)
        else:
            m = _diff_mixer(u, p, i // 2, i)
        h = h + _rms_norm(m, p['norm_post_mix'][i])
        u = _rms_norm(h, p['norm_pre_mlp'][i])
        f = jnp.square(jax.nn.relu(u @ p['w_mlp_up'][i])) @ p['w_mlp_down'][i]
        h = h + _rms_norm(f, p['norm_post_mlp'][i])
    return h


def setup_inputs(seed: int = 0) -> dict:
    key = jax.random.key(seed)
    ks = jax.random.split(key, 32)

    def nrm(k, shape, scale):
        return jax.random.normal(k, shape, jnp.float32) * scale

    def gain(k, shape):
        return 1.0 + nrm(k, shape, 0.05)

    u = jax.random.uniform(ks[9], (N_EVEN, 2, SSD_HEADS), jnp.float32)
    dt = jnp.exp(u * (math.log(0.1) - math.log(0.001)) + math.log(0.001))
    dt_bias = dt + jnp.log(-jnp.expm1(-dt))
    a_log = jnp.log(jax.random.uniform(ks[10], (N_EVEN, 2, SSD_HEADS), jnp.float32, minval=1.0, maxval=16.0))
    return {
        'x_prompt': nrm(ks[0], (BATCH, SEQ, D_MODEL), 1.0),
        'x_sample': nrm(ks[1], (DEC_BATCH, DEC_SEQ, D_MODEL), 1.0),
        'norm_pre_mix': gain(ks[2], (DEPTH, D_MODEL)),
        'norm_post_mix': gain(ks[3], (DEPTH, D_MODEL)),
        'norm_pre_mlp': gain(ks[4], (DEPTH, D_MODEL)),
        'norm_post_mlp': gain(ks[5], (DEPTH, D_MODEL)),
        'w_in_even': nrm(ks[6], (N_EVEN, D_MODEL, D_IN_EVEN), D_MODEL ** -0.5),
        'conv_w': nrm(ks[7], (N_EVEN, CONV_K, CONV_DIM), CONV_K ** -0.5),
        'conv_b': nrm(ks[8], (N_EVEN, CONV_DIM), 0.02),
        'dt_bias': dt_bias,
        'a_log': a_log,
        'd_skip': 1.0 + nrm(ks[11], (N_EVEN, SSD_HEADS), 0.1),
        'ssm_norm_w': gain(ks[12], (N_EVEN, D_SSM)),
        'q_norm_w': gain(ks[13], (N_EVEN, Q_LORA)),
        'w_q_up': nrm(ks[14], (N_EVEN, Q_LORA, MLA_HEADS * (QK_NOPE + QK_ROPE)), Q_LORA ** -0.5),
        'kv_norm_w': gain(ks[15], (N_EVEN, KV_LORA)),
        'w_kv_up': nrm(ks[16], (N_EVEN, KV_LORA, MLA_HEADS * (QK_NOPE + V_HEAD)), KV_LORA ** -0.5),
        'w_out_even': nrm(ks[17], (N_EVEN, D_MIX_EVEN, D_MODEL), D_MIX_EVEN ** -0.5),
        'w_in_odd': nrm(ks[18], (N_ODD, D_MODEL, 3 * D_DIFF), D_MODEL ** -0.5),
        'diff_lambda': nrm(ks[19], (N_ODD, 4, DIFF_HEAD_DIM), 0.1),
        'subln_w': gain(ks[20], (N_ODD, 2 * DIFF_HEAD_DIM)),
        'w_out_odd': nrm(ks[21], (N_ODD, D_DIFF, D_MODEL), D_DIFF ** -0.5),
        'w_mlp_up': nrm(ks[22], (DEPTH, D_MODEL, D_FF), D_MODEL ** -0.5),
        'w_mlp_down': nrm(ks[23], (DEPTH, D_FF, D_MODEL), D_FF ** -0.5),
    }


def reference(x_prompt, x_sample, norm_pre_mix, norm_post_mix, norm_pre_mlp, norm_post_mlp,
              w_in_even, conv_w, conv_b, dt_bias, a_log, d_skip, ssm_norm_w, q_norm_w, w_q_up,
              kv_norm_w, w_kv_up, w_out_even, w_in_odd, diff_lambda, subln_w, w_out_odd,
              w_mlp_up, w_mlp_down):
    p = {
        'norm_pre_mix': norm_pre_mix, 'norm_post_mix': norm_post_mix,
        'norm_pre_mlp': norm_pre_mlp, 'norm_post_mlp': norm_post_mlp,
        'w_in_even': w_in_even, 'conv_w': conv_w, 'conv_b': conv_b, 'dt_bias': dt_bias,
        'a_log': a_log, 'd_skip': d_skip, 'ssm_norm_w': ssm_norm_w, 'q_norm_w': q_norm_w,
        'w_q_up': w_q_up, 'kv_norm_w': kv_norm_w, 'w_kv_up': w_kv_up, 'w_out_even': w_out_even,
        'w_in_odd': w_in_odd, 'diff_lambda': diff_lambda, 'subln_w': subln_w, 'w_out_odd': w_out_odd,
        'w_mlp_up': w_mlp_up, 'w_mlp_down': w_mlp_down,
    }
    y_prompt = _trunk(x_prompt, p)
    y_sample = _trunk(x_sample, p)
    return (y_prompt, y_sample)
```

```python
import functools
import math

import jax
import jax.numpy as jnp
from jax import lax
from jax.experimental import pallas as pl
from jax.experimental.pallas import tpu as pltpu

F32 = jnp.float32
BF16 = jnp.bfloat16

EPS = 1e-6
ROPE_THETA = 500000.0
LANES = 128
VMEM_LIMIT_BYTES = 56 * 1024 * 1024

SSD_HEADS = 16
SSD_HEAD_DIM = 64
SSD_GROUPS = 2
SSD_STATE = 128
D_SSM = SSD_HEADS * SSD_HEAD_DIM
CONV_K = 5
CONV_DIM = D_SSM + 2 * SSD_GROUPS * SSD_STATE
MLA_HEADS = 8
QK_NOPE = 128
QK_ROPE = 64
V_HEAD = 128
Q_LORA = 512
KV_LORA = 512
DIFF_HEADS = 8
DIFF_HEAD_DIM = 128
DIFF_ROT = DIFF_HEAD_DIM // 4
SSD_CHUNK = 128
NEG_BIG = -1e30


def _cparams(sem):
    return pltpu.CompilerParams(dimension_semantics=sem, vmem_limit_bytes=VMEM_LIMIT_BYTES)


def _pick(n, pref):
    t = min(n, pref)
    while n % t:
        t //= 2
    return t


def _rms(x, g):
    return x * lax.rsqrt(jnp.mean(x * x, axis=-1, keepdims=True) + EPS) * g


def _rmsnorm_kernel(x_ref, g_ref, o_ref):
    o_ref[...] = _rms(x_ref[...].astype(F32), g_ref[...]).astype(o_ref.dtype)


def rmsnorm(x, g, out_dtype=BF16, tm=512):
    t, d = x.shape
    tm = _pick(t, tm)
    return pl.pallas_call(
        _rmsnorm_kernel,
        grid=(t // tm,),
        in_specs=[pl.BlockSpec((tm, d), lambda i: (i, 0)),
                  pl.BlockSpec((1, d), lambda i: (0, 0))],
        out_specs=pl.BlockSpec((tm, d), lambda i: (i, 0)),
        out_shape=jax.ShapeDtypeStruct((t, d), out_dtype),
        compiler_params=_cparams(("parallel",)),
        name="rmsnorm",
    )(x, g.reshape(1, d).astype(F32))


def _matmul_kernel(a_ref, w_ref, o_ref):
    o_ref[...] = jnp.dot(a_ref[...], w_ref[...], preferred_element_type=F32).astype(o_ref.dtype)


def matmul(a, w, out_dtype, tm=1024, tn=512):
    t, k = a.shape
    n = w.shape[1]
    tm = _pick(t, tm)
    tn = _pick(n, tn)
    return pl.pallas_call(
        _matmul_kernel,
        grid=(t // tm, n // tn),
        in_specs=[pl.BlockSpec((tm, k), lambda i, j: (i, 0)),
                  pl.BlockSpec((k, tn), lambda i, j: (0, j))],
        out_specs=pl.BlockSpec((tm, tn), lambda i, j: (i, j)),
        out_shape=jax.ShapeDtypeStruct((t, n), out_dtype),
        compiler_params=_cparams(("parallel", "parallel")),
        name="matmul",
    )(a, w)


def _outproj_kernel(*refs, n_in):
    a_refs = refs[:n_in]
    w_refs = refs[n_in:2 * n_in]
    g_ref, h_ref, o_ref = refs[2 * n_in:]
    acc = jnp.dot(a_refs[0][...], w_refs[0][...], preferred_element_type=F32)
    for a_ref, w_ref in zip(a_refs[1:], w_refs[1:]):
        acc = acc + jnp.dot(a_ref[...], w_ref[...], preferred_element_type=F32)
    o_ref[...] = h_ref[...] + _rms(acc, g_ref[...])


def outproj_postnorm_residual(acts, ws, g, h, tm=512):
    t, d = h.shape
    tm = _pick(t, tm)
    n_in = len(acts)
    in_specs = ([pl.BlockSpec((tm, a.shape[1]), lambda i: (i, 0)) for a in acts]
                + [pl.BlockSpec(w.shape, lambda i: (0, 0)) for w in ws]
                + [pl.BlockSpec((1, d), lambda i: (0, 0)),
                   pl.BlockSpec((tm, d), lambda i: (i, 0))])
    return pl.pallas_call(
        functools.partial(_outproj_kernel, n_in=n_in),
        grid=(t // tm,),
        in_specs=in_specs,
        out_specs=pl.BlockSpec((tm, d), lambda i: (i, 0)),
        out_shape=jax.ShapeDtypeStruct((t, d), F32),
        compiler_params=_cparams(("parallel",)),
        name="outproj_postnorm_residual",
    )(*acts, *ws, g.reshape(1, d).astype(F32), h)


def _mlp_kernel(h_ref, gpre_ref, wup_ref, wdn_ref, gpost_ref, o_ref, xn_ref, acc_ref):
    j = pl.program_id(1)

    @pl.when(j == 0)
    def _():
        xn_ref[...] = _rms(h_ref[...], gpre_ref[...]).astype(BF16)
        acc_ref[...] = jnp.zeros_like(acc_ref)

    a = jnp.dot(xn_ref[...], wup_ref[...], preferred_element_type=F32)
    a = jnp.square(jnp.maximum(a, 0.0)).astype(BF16)
    acc_ref[...] += jnp.dot(a, wdn_ref[...], preferred_element_type=F32)

    @pl.when(j == pl.num_programs(1) - 1)
    def _():
        o_ref[...] = h_ref[...] + _rms(acc_ref[...], gpost_ref[...])


def mlp_block(h, g_pre, w_up, w_down, g_post, tm=512, tf=512):
    t, d = h.shape
    f = w_up.shape[1]
    tm = _pick(t, tm)
    tf = _pick(f, tf)
    return pl.pallas_call(
        _mlp_kernel,
        grid=(t // tm, f // tf),
        in_specs=[pl.BlockSpec((tm, d), lambda i, j: (i, 0)),
                  pl.BlockSpec((1, d), lambda i, j: (0, 0)),
                  pl.BlockSpec((d, tf), lambda i, j: (0, j)),
                  pl.BlockSpec((tf, d), lambda i, j: (j, 0)),
                  pl.BlockSpec((1, d), lambda i, j: (0, 0))],
        out_specs=pl.BlockSpec((tm, d), lambda i, j: (i, 0)),
        out_shape=jax.ShapeDtypeStruct((t, d), F32),
        scratch_shapes=[pltpu.VMEM((tm, d), BF16), pltpu.VMEM((tm, d), F32)],
        compiler_params=_cparams(("parallel", "arbitrary")),
        name="mlp_block",
    )(h, g_pre.reshape(1, d).astype(F32), w_up, w_down, g_post.reshape(1, d).astype(F32))


class SeqLayout:
    def __init__(self, n0, l0, n1, l1):
        self.n0, self.l0, self.n1, self.l1 = n0, l0, n1, l1
        self.r0 = n0 * l0
        self.t = self.r0 + n1 * l1
        assert self.r0 % l1 == 0, "sample sequences must start on a multiple of their length"

    def pos_and_len(self, row):
        in0 = row < self.r0
        pos = jnp.where(in0, lax.rem(row, self.l0), lax.rem(row - self.r0, self.l1))
        return pos, jnp.where(in0, self.l0, self.l1)

    def table_block(self, i, tm):
        b0 = self.r0 // tm
        return jnp.where(i < b0, lax.rem(i, self.l0 // tm), lax.rem(i - b0, self.l1 // tm))


def rope_tables(half, n_pos):
    inv_freq = ROPE_THETA ** (-jnp.arange(half, dtype=F32) / half)
    ang = jnp.arange(n_pos, dtype=F32)[:, None] * inv_freq[None, :]
    cos, sin = jnp.cos(ang), jnp.sin(ang)
    one = jnp.ones((n_pos, LANES - 2 * half), F32)
    zero = jnp.zeros((n_pos, LANES - 2 * half), F32)
    zh = jnp.zeros((n_pos, half), F32)
    c = jnp.concatenate([cos, cos, one], axis=1)
    sa = jnp.concatenate([-sin, zh, zero], axis=1)
    sb = jnp.concatenate([zh, sin, zero], axis=1)
    return c, sa, sb


def _rope_group(x, c, sa, sb, half):
    return (x * c + pltpu.roll(x, LANES - half, axis=1) * sa + pltpu.roll(x, half, axis=1) * sb)


CONV_HALO = 16


def _conv_kernel(main_ref, prev_ref, next_ref, w_ref, b_ref, o_ref, buf_ref, *, layout, tq):
    i = pl.program_id(0)
    pos, seqlen = layout.pos_and_len(i * tq)
    first = pos == 0
    last = pos + tq == seqlen
    buf_ref[0:CONV_HALO, :] = jnp.where(first, 0.0, prev_ref[...].astype(F32))
    buf_ref[CONV_HALO:CONV_HALO + tq, :] = main_ref[...].astype(F32)
    buf_ref[CONV_HALO + tq:, :] = jnp.where(last, 0.0, next_ref[...].astype(F32))
    acc = jnp.zeros(o_ref.shape, F32) + b_ref[...]
    for k in range(CONV_K):
        start = CONV_HALO + k - CONV_K // 2
        acc = acc + buf_ref[start:start + tq, :] * w_ref[k:k + 1, :]
    o_ref[...] = (acc * (1.0 / (1.0 + jnp.exp(-acc)))).astype(o_ref.dtype)


def conv_silu(zx, col0, conv_w, conv_b, layout, tq=256, tc=512):
    t = zx.shape[0]
    c = conv_w.shape[1]
    tq = _pick(min(layout.l0, layout.l1), tq)
    cb0 = col0 // tc
    hb = tq // CONV_HALO
    nh = t // CONV_HALO
    return pl.pallas_call(
        functools.partial(_conv_kernel, layout=layout, tq=tq),
        grid=(t // tq, c // tc),
        in_specs=[pl.BlockSpec((tq, tc), lambda i, j: (i, cb0 + j)),
                  pl.BlockSpec((CONV_HALO, tc), lambda i, j: (jnp.maximum(i * hb - 1, 0), cb0 + j)),
                  pl.BlockSpec((CONV_HALO, tc), lambda i, j: (jnp.minimum((i + 1) * hb, nh - 1), cb0 + j)),
                  pl.BlockSpec((CONV_K, tc), lambda i, j: (0, j)),
                  pl.BlockSpec((1, tc), lambda i, j: (0, j))],
        out_specs=pl.BlockSpec((tq, tc), lambda i, j: (i, j)),
        out_shape=jax.ShapeDtypeStruct((t, c), BF16),
        scratch_shapes=[pltpu.VMEM((tq + 2 * CONV_HALO, tc), F32)],
        compiler_params=_cparams(("parallel", "parallel")),
        name="conv_silu",
    )(zx, zx, zx, conv_w.astype(F32), conv_b.reshape(1, c).astype(F32))


def _split3(x):
    hi = x.astype(BF16)
    r1 = x - hi.astype(F32)
    mid = r1.astype(BF16)
    lo = (r1 - mid.astype(F32)).astype(BF16)
    return hi, mid, lo


def _expand_heads(v, e_ref):
    hi, mid, lo = _split3(v)
    e = e_ref[...]
    return (jnp.dot(hi, e, preferred_element_type=F32) + jnp.dot(mid, e, preferred_element_type=F32)
            + jnp.dot(lo, e, preferred_element_type=F32))


def _softplus(x):
    return jnp.maximum(x, 0.0) + jnp.log1p(jnp.exp(-jnp.abs(x)))


def _ssd_kernel(x_ref, b_ref, c_ref, dt_ref, dtt_ref, bias_ref, biast_ref, alog_ref, alogt_ref,
                e_ref, y_ref, s_ref, *, layout, rev):
    q = SSD_CHUNK
    hpg = SSD_HEADS // SSD_GROUPS
    i = pl.program_id(0)
    nchunks = pl.num_programs(0)
    chunk = (nchunks - 1 - i) if rev else i
    pos, seqlen = layout.pos_and_len(chunk * q)
    starts = (pos + q == seqlen) if rev else (pos == 0)

    @pl.when(starts)
    def _():
        s_ref[...] = jnp.zeros_like(s_ref)

    h0 = SSD_HEADS if rev else 0
    dt = _softplus(dt_ref[...] + bias_ref[...])
    da = dt * (-jnp.exp(alog_ref[...]))
    dt_t = _softplus(dtt_ref[...] + biast_ref[...])
    da_t = dt_t * (-jnp.exp(alogt_ref[...]))

    row = lax.broadcasted_iota(jnp.int32, (q, q), 0)
    col = lax.broadcasted_iota(jnp.int32, (q, q), 1)
    mask = (col >= row) if rev else (col <= row)
    tri = jnp.where(mask, 1.0, 0.0).astype(BF16)
    tri_t = jnp.where((row >= col) if rev else (row <= col), 1.0, 0.0).astype(BF16)

    hi, mid, lo = _split3(da)
    cum = (jnp.dot(tri, hi, preferred_element_type=F32) + jnp.dot(tri, mid, preferred_element_type=F32)
           + jnp.dot(tri, lo, preferred_element_type=F32))
    hi, mid, lo = _split3(da_t)
    cum_t = (jnp.dot(hi, tri_t, preferred_element_type=F32) + jnp.dot(mid, tri_t, preferred_element_type=F32)
             + jnp.dot(lo, tri_t, preferred_element_type=F32))

    end = 0 if rev else q - 1
    cum_end = cum[end:end + 1, :]
    ecum = jnp.exp(cum)
    wgt = dt * jnp.exp(cum_end - cum)

    xb = x_ref[...]
    bm = b_ref[...]
    cm = c_ref[...]
    s_in = s_ref[...].astype(BF16)

    for g in range(SSD_GROUPS):
        bg = bm[:, g * SSD_STATE:(g + 1) * SSD_STATE]
        cg = cm[:, g * SSD_STATE:(g + 1) * SSD_STATE]
        cb = lax.dot_general(cg, bg, (((1,), (1,)), ((), ())), preferred_element_type=F32)
        cg32 = cg.astype(F32)
        for pair in range(hpg // 2):
            lane0 = (g * hpg + 2 * pair) * SSD_HEAD_DIM
            rhs = jnp.concatenate([xb[:, lane0:lane0 + LANES], s_in[:, lane0:lane0 + LANES]], axis=0)
            res = []
            for sub in range(2):
                h = h0 + g * hpg + 2 * pair + sub
                diff = cum[:, h:h + 1] - cum_t[h:h + 1, :]
                dec = jnp.exp(jnp.where(mask, diff, NEG_BIG))
                m = (cb * dec * dt_t[h:h + 1, :]).astype(BF16)
                ce = (cg32 * ecum[:, h:h + 1]).astype(BF16)
                lhs = jnp.concatenate([m, ce], axis=1)
                res.append(jnp.dot(lhs, rhs, preferred_element_type=F32))
            lane = lax.broadcasted_iota(jnp.int32, (q, LANES), 1)
            y_ref[:, lane0:lane0 + LANES] = jnp.where(lane < SSD_HEAD_DIM, res[0], res[1])

    xw = (xb.astype(F32) * _expand_heads(wgt, e_ref)).astype(BF16)
    decay8 = _expand_heads(jnp.exp(cum[end - end % 8:end - end % 8 + 8, :]), e_ref)
    decay = decay8[end % 8:end % 8 + 1, :]
    gw = SSD_HEADS // SSD_GROUPS * SSD_HEAD_DIM
    for g in range(SSD_GROUPS):
        bg_t = bm[:, g * SSD_STATE:(g + 1) * SSD_STATE].astype(F32).T.astype(BF16)
        upd = jnp.dot(bg_t, xw[:, g * gw:(g + 1) * gw], preferred_element_type=F32)
        s_ref[:, g * gw:(g + 1) * gw] = s_ref[:, g * gw:(g + 1) * gw] * decay[:, g * gw:(g + 1) * gw] + upd


def ssd_scan(xbc, small, dt_col0, dt_t, dt_bias, a_log, layout, rev):
    t = xbc.shape[0]
    q = SSD_CHUNK
    n = t // q
    ci = (lambda i: n - 1 - i) if rev else (lambda i: i)
    bc_w = SSD_GROUPS * SSD_STATE
    h0 = SSD_HEADS if rev else 0
    expand = jnp.zeros((LANES, D_SSM), BF16).at[h0:h0 + SSD_HEADS].set(
        jnp.repeat(jnp.eye(SSD_HEADS, dtype=BF16), SSD_HEAD_DIM, axis=1))
    full = lambda a: pl.BlockSpec(a.shape, lambda i: (0,) * a.ndim)
    pad = jnp.zeros((LANES - 2 * SSD_HEADS,), F32)
    bias = jnp.concatenate([dt_bias.astype(F32).reshape(-1), pad]).reshape(1, LANES)
    alog = jnp.concatenate([a_log.astype(F32).reshape(-1), pad]).reshape(1, LANES)
    bias_t = dt_bias.astype(F32).reshape(2 * SSD_HEADS, 1)
    alog_t = a_log.astype(F32).reshape(2 * SSD_HEADS, 1)
    args = (xbc, xbc, xbc, small, dt_t, bias, bias_t, alog, alog_t, expand)
    return pl.pallas_call(
        functools.partial(_ssd_kernel, layout=layout, rev=rev),
        grid=(n,),
        in_specs=[pl.BlockSpec((q, D_SSM), lambda i: (ci(i), 0)),
                  pl.BlockSpec((q, bc_w), lambda i: (ci(i), D_SSM // bc_w)),
                  pl.BlockSpec((q, bc_w), lambda i: (ci(i), D_SSM // bc_w + 1)),
                  pl.BlockSpec((q, LANES), lambda i: (ci(i), dt_col0 // LANES)),
                  pl.BlockSpec((2 * SSD_HEADS, q), lambda i: (0, ci(i))),
                  full(bias), full(bias_t), full(alog), full(alog_t), full(expand)],
        out_specs=pl.BlockSpec((q, D_SSM), lambda i: (ci(i), 0)),
        out_shape=jax.ShapeDtypeStruct((t, D_SSM), F32),
        scratch_shapes=[pltpu.VMEM((SSD_STATE, D_SSM), F32)],
        compiler_params=_cparams(("arbitrary",)),
        name="ssd_scan_bwd" if rev else "ssd_scan_fwd",
    )(*args)


def _ssd_combine_kernel(yf_ref, yb_ref, x_ref, z_ref, dskip_ref, g_ref, o_ref):
    x = x_ref[...].astype(F32)
    z = z_ref[...].astype(F32)
    y = (yf_ref[...] + yb_ref[...] + x * dskip_ref[...]) * (z * (1.0 / (1.0 + jnp.exp(-z))))
    gw = D_SSM // SSD_GROUPS
    for g in range(SSD_GROUPS):
        sl = slice(g * gw, (g + 1) * gw)
        o_ref[:, sl] = _rms(y[:, sl], g_ref[:, sl]).astype(o_ref.dtype)


def ssd_combine(y_f, y_b, xbc, zx, d_skip, norm_w, tm=512):
    t = y_f.shape[0]
    tm = _pick(t, tm)
    dvec = jnp.repeat(d_skip.astype(F32), SSD_HEAD_DIM).reshape(1, D_SSM)
    row = pl.BlockSpec((tm, D_SSM), lambda i: (i, 0))
    vec = pl.BlockSpec((1, D_SSM), lambda i: (0, 0))
    return pl.pallas_call(
        _ssd_combine_kernel,
        grid=(t // tm,),
        in_specs=[row, row, row, row, vec, vec],
        out_specs=row,
        out_shape=jax.ShapeDtypeStruct((t, D_SSM), BF16),
        compiler_params=_cparams(("parallel",)),
        name="ssd_combine",
    )(y_f, y_b, xbc, zx, dvec, norm_w.reshape(1, D_SSM).astype(F32))


def _mla_prep_kernel(cq_ref, ckv_ref, kpe_ref, gq_ref, gkv_ref, wq_ref, wkv_ref, c_ref, sa_ref, sb_ref,
                     q_ref, k_ref, v_ref, *, scale):
    half = QK_ROPE // 2
    c, sa, sb = c_ref[...], sa_ref[...], sb_ref[...]
    nw = MLA_HEADS * QK_NOPE
    qn = _rms(cq_ref[...], gq_ref[...]).astype(BF16)
    q = jnp.dot(qn, wq_ref[...], preferred_element_type=F32) * scale
    kvn = _rms(ckv_ref[...], gkv_ref[...]).astype(BF16)
    kv = jnp.dot(kvn, wkv_ref[...], preferred_element_type=F32)
    kpe = _rope_group(kpe_ref[...], c, sa, sb, half).astype(BF16)
    for h in range(MLA_HEADS):
        lo = h * 2 * LANES
        q_ref[:, lo:lo + LANES] = q[:, h * LANES:(h + 1) * LANES].astype(BF16)
        qpe = q[:, nw + h * LANES:nw + (h + 1) * LANES]
        q_ref[:, lo + LANES:lo + 2 * LANES] = _rope_group(qpe, c, sa, sb, half).astype(BF16)
        k_ref[:, lo:lo + LANES] = kv[:, h * LANES:(h + 1) * LANES].astype(BF16)
        k_ref[:, lo + LANES:lo + 2 * LANES] = kpe
    v_ref[...] = kv[:, nw:].astype(BF16)


def mla_prep(small, gq, gkv, wq, wkv, tabs, layout, tm=256):
    t = small.shape[0]
    tm = _pick(min(layout.l0, layout.l1), tm)
    hq = MLA_HEADS * 2 * LANES
    tab = pl.BlockSpec((tm, LANES), lambda i: (layout.table_block(i, tm), 0))
    full = lambda a: pl.BlockSpec(a.shape, lambda i: (0,) * a.ndim)
    gq2 = gq.reshape(1, Q_LORA).astype(F32)
    gkv2 = gkv.reshape(1, KV_LORA).astype(F32)
    return pl.pallas_call(
        functools.partial(_mla_prep_kernel, scale=1.0 / math.sqrt(QK_NOPE + QK_ROPE)),
        grid=(t // tm,),
        in_specs=[pl.BlockSpec((tm, Q_LORA), lambda i: (i, 0)),
                  pl.BlockSpec((tm, KV_LORA), lambda i: (i, 1)),
                  pl.BlockSpec((tm, LANES), lambda i: (i, (Q_LORA + KV_LORA) // LANES)),
                  full(gq2), full(gkv2), full(wq), full(wkv), tab, tab, tab],
        out_specs=[pl.BlockSpec((tm, hq), lambda i: (i, 0)),
                   pl.BlockSpec((tm, hq), lambda i: (i, 0)),
                   pl.BlockSpec((tm, MLA_HEADS * V_HEAD), lambda i: (i, 0))],
        out_shape=[jax.ShapeDtypeStruct((t, hq), BF16),
                   jax.ShapeDtypeStruct((t, hq), BF16),
                   jax.ShapeDtypeStruct((t, MLA_HEADS * V_HEAD), BF16)],
        compiler_params=_cparams(("parallel",)),
        name="mla_prep",
    )(small, small, small, gq2, gkv2, wq, wkv, *tabs)


def _rope_cols_kernel(x_ref, c_ref, sa_ref, sb_ref, o_ref, *, half, scale_groups, scale):
    c, sa, sb = c_ref[...], sa_ref[...], sb_ref[...]
    j = pl.program_id(1)
    n = x_ref.shape[1] // LANES
    f = jnp.where(j * n < scale_groups, scale, 1.0)
    for gidx in range(n):
        sl = slice(gidx * LANES, (gidx + 1) * LANES)
        o_ref[:, sl] = (_rope_group(x_ref[:, sl], c, sa, sb, half) * f).astype(o_ref.dtype)


def rope_cols(x, ncols, tabs, half, scale_groups, scale, layout, tm=256, tc=1024):
    t = x.shape[0]
    tm = _pick(min(layout.l0, layout.l1), tm)
    tc = _pick(ncols, tc)
    assert scale_groups % (tc // LANES) == 0
    tab = pl.BlockSpec((tm, LANES), lambda i, j: (layout.table_block(i, tm), 0))
    return pl.pallas_call(
        functools.partial(_rope_cols_kernel, half=half, scale_groups=scale_groups, scale=scale),
        grid=(t // tm, ncols // tc),
        in_specs=[pl.BlockSpec((tm, tc), lambda i, j: (i, j)), tab, tab, tab],
        out_specs=pl.BlockSpec((tm, tc), lambda i, j: (i, j)),
        out_shape=jax.ShapeDtypeStruct((t, ncols), BF16),
        compiler_params=_cparams(("parallel", "parallel")),
        name="rope_cols",
    )(x, *tabs)


def _flash_kernel(*refs, n_maps, seq_len, bk, lam_init):
    if n_maps == 2:
        q_ref, k_ref, v_ref, lam_ref, subln_ref, o_ref, m_ref, l_ref, acc_ref = refs
    else:
        q_ref, k_ref, v_ref, o_ref, m_ref, l_ref, acc_ref = refs
    dqk = q_ref.shape[1] // n_maps
    m_ref[...] = jnp.full_like(m_ref, NEG_BIG)
    l_ref[...] = jnp.zeros_like(l_ref)
    acc_ref[...] = jnp.zeros_like(acc_ref)
    qs = [q_ref[:, t * dqk:(t + 1) * dqk] for t in range(n_maps)]

    def body(j, carry):
        off = pl.multiple_of(j * bk, bk)
        k = k_ref[pl.ds(off, bk), :]
        v = v_ref[pl.ds(off, bk), :]
        for t in range(n_maps):
            s = lax.dot_general(qs[t], k[:, t * dqk:(t + 1) * dqk], (((1,), (1,)), ((), ())),
                                preferred_element_type=F32)
            m_old = m_ref[t]
            m_new = jnp.maximum(m_old, jnp.max(s, axis=-1, keepdims=True))
            alpha = jnp.exp(m_old - m_new)
            p = jnp.exp(s - m_new)
            l_ref[t] = alpha * l_ref[t] + jnp.sum(p, axis=-1, keepdims=True)
            acc_ref[t] = alpha * acc_ref[t] + jnp.dot(p.astype(BF16), v, preferred_element_type=F32)
            m_ref[t] = m_new
        return carry

    lax.fori_loop(0, seq_len // bk, body, 0)

    if n_maps == 1:
        o_ref[...] = (acc_ref[0] / l_ref[0]).astype(o_ref.dtype)
    else:
        lp = lam_ref[...]
        lam = (jnp.exp(jnp.sum(lp[0:1] * lp[1:2], axis=-1, keepdims=True))
               - jnp.exp(jnp.sum(lp[2:3] * lp[3:4], axis=-1, keepdims=True)) + lam_init)
        o = acc_ref[0] / l_ref[0] - lam * (acc_ref[1] / l_ref[1])
        o_ref[...] = (_rms(o, subln_ref[...]) * (1.0 - lam_init)).astype(o_ref.dtype)


def flash_attention(q, k, v, n_heads, n_maps, q_col0, k_col0, v_col0, row0, n_seq, seq_len,
                    lam=None, subln=None, lam_init=0.0, bq=512, bk=512):
    hq = 2 * LANES
    dv = LANES * n_maps
    bq = _pick(seq_len, bq)
    bk = _pick(seq_len, bk)
    nq = seq_len // bq
    assert row0 % seq_len == 0 and q_col0 % hq == 0 and k_col0 % hq == 0 and v_col0 % dv == 0
    qb0, sb0 = row0 // bq, row0 // seq_len
    in_specs = [pl.BlockSpec((bq, hq), lambda b, h, i: (qb0 + b * nq + i, q_col0 // hq + h)),
                pl.BlockSpec((seq_len, hq), lambda b, h, i: (sb0 + b, k_col0 // hq + h)),
                pl.BlockSpec((seq_len, dv), lambda b, h, i: (sb0 + b, v_col0 // dv + h))]
    args = [q, k, v]
    if n_maps == 2:
        in_specs += [pl.BlockSpec(lam.shape, lambda b, h, i: (0, 0)),
                     pl.BlockSpec(subln.shape, lambda b, h, i: (0, 0))]
        args += [lam, subln]
    return pl.pallas_call(
        functools.partial(_flash_kernel, n_maps=n_maps, seq_len=seq_len, bk=bk, lam_init=lam_init),
        grid=(n_seq, n_heads, nq),
        in_specs=in_specs,
        out_specs=pl.BlockSpec((bq, dv), lambda b, h, i: (b * nq + i, h)),
        out_shape=jax.ShapeDtypeStruct((n_seq * seq_len, n_heads * dv), BF16),
        scratch_shapes=[pltpu.VMEM((n_maps, bq, 1), F32), pltpu.VMEM((n_maps, bq, 1), F32),
                        pltpu.VMEM((n_maps, bq, dv), F32)],
        compiler_params=_cparams(("parallel", "parallel", "parallel")),
        name="diff_attention" if n_maps == 2 else "mla_attention",
    )(*args)


def attention_all(layout, **kw):
    parts = []
    if layout.n0:
        parts.append(flash_attention(row0=0, n_seq=layout.n0, seq_len=layout.l0, **kw))
    if layout.n1:
        parts.append(flash_attention(row0=layout.r0, n_seq=layout.n1, seq_len=layout.l1, **kw))
    return jnp.concatenate(parts, axis=0)


def _even_in_weight(w):
    o1 = D_SSM
    o2 = o1 + CONV_DIM
    o3 = o2 + 2 * SSD_HEADS
    o4 = o3 + Q_LORA
    o5 = o4 + KV_LORA
    d = w.shape[0]
    zeros = lambda n: jnp.zeros((d, n), w.dtype)
    main = w[:, :o2]
    small = jnp.concatenate([w[:, o3:o4], w[:, o4:o5], w[:, o5:], zeros(LANES - QK_ROPE),
                             w[:, o2:o3], zeros(LANES - 2 * SSD_HEADS)], axis=1)
    return main.astype(BF16), small.astype(BF16)


def _mla_up_weights(w_q_up, w_kv_up):
    wq = w_q_up.reshape(Q_LORA, MLA_HEADS, QK_NOPE + QK_ROPE)
    q_nope = wq[:, :, :QK_NOPE].reshape(Q_LORA, MLA_HEADS * QK_NOPE)
    q_pe = jnp.pad(wq[:, :, QK_NOPE:], ((0, 0), (0, 0), (0, LANES - QK_ROPE))).reshape(Q_LORA, MLA_HEADS * LANES)
    wkv = w_kv_up.reshape(KV_LORA, MLA_HEADS, QK_NOPE + V_HEAD)
    k_nope = wkv[:, :, :QK_NOPE].reshape(KV_LORA, MLA_HEADS * QK_NOPE)
    v = wkv[:, :, QK_NOPE:].reshape(KV_LORA, MLA_HEADS * V_HEAD)
    return (jnp.concatenate([q_nope, q_pe], axis=1).astype(BF16),
            jnp.concatenate([k_nope, v], axis=1).astype(BF16))


def _even_mixer(u, p, i, layout, tabs):
    w_main, w_small = _even_in_weight(p['w_in_even'][i])
    zx = matmul(u, w_main, BF16)
    small = matmul(u, w_small, F32)
    dt_col0 = Q_LORA + KV_LORA + LANES

    xbc = conv_silu(zx, D_SSM, p['conv_w'][i], p['conv_b'][i], layout)
    dt_t = small[:, dt_col0:dt_col0 + 2 * SSD_HEADS].T
    y_f = ssd_scan(xbc, small, dt_col0, dt_t, p['dt_bias'][i], p['a_log'][i], layout, rev=False)
    y_b = ssd_scan(xbc, small, dt_col0, dt_t, p['dt_bias'][i], p['a_log'][i], layout, rev=True)
    y = ssd_combine(y_f, y_b, xbc, zx, p['d_skip'][i], p['ssm_norm_w'][i])

    wq, wkv = _mla_up_weights(p['w_q_up'][i], p['w_kv_up'][i])
    q, k, v = mla_prep(small, p['q_norm_w'][i], p['kv_norm_w'][i], wq, wkv, tabs, layout)
    o = attention_all(layout, q=q, k=k, v=v, n_heads=MLA_HEADS, n_maps=1, q_col0=0, k_col0=0, v_col0=0)
    w_out = p['w_out_even'][i].astype(BF16)
    return (y, o), (w_out[:D_SSM], w_out[D_SSM:])


def _diff_mixer(u, p, j, layer_idx, layout, tabs):
    d_diff = DIFF_HEADS * 2 * DIFF_HEAD_DIM
    w_in = p['w_in_odd'][j].astype(BF16)
    qk = matmul(u, w_in[:, :2 * d_diff], F32)
    v = matmul(u, w_in[:, 2 * d_diff:], BF16)
    qk = rope_cols(qk, 2 * d_diff, tabs, DIFF_ROT // 2, d_diff // LANES, 1.0 / math.sqrt(DIFF_HEAD_DIM), layout)
    lam_init = 0.8 - 0.6 * math.exp(-0.3 * layer_idx)
    o = attention_all(layout, q=qk, k=qk, v=v, n_heads=DIFF_HEADS, n_maps=2, q_col0=0, k_col0=d_diff,
                      v_col0=0, lam=p['diff_lambda'][j].astype(F32),
                      subln=p['subln_w'][j].reshape(1, -1).astype(F32), lam_init=lam_init)
    return (o,), (p['w_out_odd'][j].astype(BF16),)


def kernel(x_prompt, x_sample, norm_pre_mix, norm_post_mix, norm_pre_mlp, norm_post_mlp,
           w_in_even, conv_w, conv_b, dt_bias, a_log, d_skip, ssm_norm_w, q_norm_w, w_q_up,
           kv_norm_w, w_kv_up, w_out_even, w_in_odd, diff_lambda, subln_w, w_out_odd,
           w_mlp_up, w_mlp_down):
    p = dict(w_in_even=w_in_even, conv_w=conv_w, conv_b=conv_b, dt_bias=dt_bias, a_log=a_log,
             d_skip=d_skip, ssm_norm_w=ssm_norm_w, q_norm_w=q_norm_w, w_q_up=w_q_up,
             kv_norm_w=kv_norm_w, w_kv_up=w_kv_up, w_out_even=w_out_even, w_in_odd=w_in_odd,
             diff_lambda=diff_lambda, subln_w=subln_w, w_out_odd=w_out_odd)
    n0, l0, d = x_prompt.shape
    n1, l1, _ = x_sample.shape
    layout = SeqLayout(n0, l0, n1, l1)
    depth = norm_pre_mix.shape[0]
    n_pos = max(l0, l1)
    tabs_mla = rope_tables(QK_ROPE // 2, n_pos)
    tabs_diff = rope_tables(DIFF_ROT // 2, n_pos)

    h = jnp.concatenate([x_prompt.reshape(n0 * l0, d), x_sample.reshape(n1 * l1, d)], axis=0)
    for i in range(depth):
        u = rmsnorm(h, norm_pre_mix[i])
        if i % 2 == 0:
            acts, ws = _even_mixer(u, p, i // 2, layout, tabs_mla)
        else:
            acts, ws = _diff_mixer(u, p, i // 2, i, layout, tabs_diff)
        h = outproj_postnorm_residual(acts, ws, norm_post_mix[i], h)
        h = mlp_block(h, norm_pre_mlp[i], w_mlp_up[i].astype(BF16), w_mlp_down[i].astype(BF16),
                      norm_post_mlp[i])
    return (h[:layout.r0].reshape(n0, l0, d), h[layout.r0:].reshape(n1, l1, d))
```

```python
import functools
import math

import jax
import jax.numpy as jnp
from jax import lax
from jax.experimental import pallas as pl
from jax.experimental.pallas import tpu as pltpu

F32 = jnp.float32
BF16 = jnp.bfloat16

EPS = 1e-6
ROPE_THETA = 500000.0
LANES = 128
VMEM_LIMIT_BYTES = 56 * 1024 * 1024

SSD_HEADS = 16
SSD_HEAD_DIM = 64
SSD_GROUPS = 2
SSD_STATE = 128
D_SSM = SSD_HEADS * SSD_HEAD_DIM
CONV_K = 5
CONV_DIM = D_SSM + 2 * SSD_GROUPS * SSD_STATE
MLA_HEADS = 8
QK_NOPE = 128
QK_ROPE = 64
V_HEAD = 128
Q_LORA = 512
KV_LORA = 512
DIFF_HEADS = 8
DIFF_HEAD_DIM = 128
DIFF_ROT = DIFF_HEAD_DIM // 4
SSD_CHUNK = 128
NEG_BIG = -1e30
LOG2_E = math.log2(math.e)


def _cparams(sem):
    return pltpu.CompilerParams(dimension_semantics=sem, vmem_limit_bytes=VMEM_LIMIT_BYTES)


def _pick(n, pref):
    t = min(n, pref)
    while n % t:
        t //= 2
    return t


def _rms(x, g):
    return x * lax.rsqrt(jnp.mean(x * x, axis=-1, keepdims=True) + EPS) * g


def _rmsnorm_kernel(x_ref, g_ref, o_ref):
    o_ref[...] = _rms(x_ref[...].astype(F32), g_ref[...]).astype(o_ref.dtype)


def rmsnorm(x, g, out_dtype=BF16, tm=512):
    t, d = x.shape
    tm = _pick(t, tm)
    return pl.pallas_call(
        _rmsnorm_kernel,
        grid=(t // tm,),
        in_specs=[pl.BlockSpec((tm, d), lambda i: (i, 0)),
                  pl.BlockSpec((1, d), lambda i: (0, 0))],
        out_specs=pl.BlockSpec((tm, d), lambda i: (i, 0)),
        out_shape=jax.ShapeDtypeStruct((t, d), out_dtype),
        compiler_params=_cparams(("parallel",)),
        name="rmsnorm",
    )(x, g.reshape(1, d).astype(F32))


def _matmul_kernel(a_ref, w_ref, o_ref):
    o_ref[...] = jnp.dot(a_ref[...], w_ref[...], preferred_element_type=F32).astype(o_ref.dtype)


def matmul(a, w, out_dtype, tm=1024, tn=512):
    t, k = a.shape
    n = w.shape[1]
    tm = _pick(t, tm)
    tn = _pick(n, tn)
    return pl.pallas_call(
        _matmul_kernel,
        grid=(t // tm, n // tn),
        in_specs=[pl.BlockSpec((tm, k), lambda i, j: (i, 0)),
                  pl.BlockSpec((k, tn), lambda i, j: (0, j))],
        out_specs=pl.BlockSpec((tm, tn), lambda i, j: (i, j)),
        out_shape=jax.ShapeDtypeStruct((t, n), out_dtype),
        compiler_params=_cparams(("parallel", "parallel")),
        name="matmul",
    )(a, w)


def _outproj_kernel(*refs, n_in):
    a_refs = refs[:n_in]
    w_refs = refs[n_in:2 * n_in]
    g_ref, h_ref, o_ref = refs[2 * n_in:]
    acc = jnp.dot(a_refs[0][...], w_refs[0][...], preferred_element_type=F32)
    for a_ref, w_ref in zip(a_refs[1:], w_refs[1:]):
        acc = acc + jnp.dot(a_ref[...], w_ref[...], preferred_element_type=F32)
    o_ref[...] = h_ref[...] + _rms(acc, g_ref[...])


def outproj_postnorm_residual(acts, ws, g, h, tm=512):
    t, d = h.shape
    tm = _pick(t, tm)
    n_in = len(acts)
    in_specs = ([pl.BlockSpec((tm, a.shape[1]), lambda i: (i, 0)) for a in acts]
                + [pl.BlockSpec(w.shape, lambda i: (0, 0)) for w in ws]
                + [pl.BlockSpec((1, d), lambda i: (0, 0)),
                   pl.BlockSpec((tm, d), lambda i: (i, 0))])
    return pl.pallas_call(
        functools.partial(_outproj_kernel, n_in=n_in),
        grid=(t // tm,),
        in_specs=in_specs,
        out_specs=pl.BlockSpec((tm, d), lambda i: (i, 0)),
        out_shape=jax.ShapeDtypeStruct((t, d), F32),
        compiler_params=_cparams(("parallel",)),
        name="outproj_postnorm_residual",
    )(*acts, *ws, g.reshape(1, d).astype(F32), h)


def _mlp_kernel(h_ref, gpre_ref, wup_ref, wdn_ref, gpost_ref, o_ref, xn_ref, acc_ref):
    j = pl.program_id(1)

    @pl.when(j == 0)
    def _():
        xn_ref[...] = _rms(h_ref[...], gpre_ref[...]).astype(BF16)
        acc_ref[...] = jnp.zeros_like(acc_ref)

    a = jnp.dot(xn_ref[...], wup_ref[...], preferred_element_type=F32)
    a = jnp.square(jnp.maximum(a, 0.0)).astype(BF16)
    acc_ref[...] += jnp.dot(a, wdn_ref[...], preferred_element_type=F32)

    @pl.when(j == pl.num_programs(1) - 1)
    def _():
        o_ref[...] = h_ref[...] + _rms(acc_ref[...], gpost_ref[...])


def mlp_block(h, g_pre, w_up, w_down, g_post, tm=512, tf=512):
    t, d = h.shape
    f = w_up.shape[1]
    tm = _pick(t, tm)
    tf = _pick(f, tf)
    return pl.pallas_call(
        _mlp_kernel,
        grid=(t // tm, f // tf),
        in_specs=[pl.BlockSpec((tm, d), lambda i, j: (i, 0)),
                  pl.BlockSpec((1, d), lambda i, j: (0, 0)),
                  pl.BlockSpec((d, tf), lambda i, j: (0, j)),
                  pl.BlockSpec((tf, d), lambda i, j: (j, 0)),
                  pl.BlockSpec((1, d), lambda i, j: (0, 0))],
        out_specs=pl.BlockSpec((tm, d), lambda i, j: (i, 0)),
        out_shape=jax.ShapeDtypeStruct((t, d), F32),
        scratch_shapes=[pltpu.VMEM((tm, d), BF16), pltpu.VMEM((tm, d), F32)],
        compiler_params=_cparams(("parallel", "arbitrary")),
        name="mlp_block",
    )(h, g_pre.reshape(1, d).astype(F32), w_up, w_down, g_post.reshape(1, d).astype(F32))


class SeqLayout:
    def __init__(self, n0, l0, n1, l1):
        self.n0, self.l0, self.n1, self.l1 = n0, l0, n1, l1
        self.r0 = n0 * l0
        self.t = self.r0 + n1 * l1
        assert self.r0 % l1 == 0, "sample sequences must start on a multiple of their length"

    def pos_and_len(self, row):
        in0 = row < self.r0
        pos = jnp.where(in0, lax.rem(row, self.l0), lax.rem(row - self.r0, self.l1))
        return pos, jnp.where(in0, self.l0, self.l1)

    def table_block(self, i, tm):
        b0 = self.r0 // tm
        return jnp.where(i < b0, lax.rem(i, self.l0 // tm), lax.rem(i - b0, self.l1 // tm))


def rope_tables(half, n_pos):
    inv_freq = ROPE_THETA ** (-jnp.arange(half, dtype=F32) / half)
    ang = jnp.arange(n_pos, dtype=F32)[:, None] * inv_freq[None, :]
    cos, sin = jnp.cos(ang), jnp.sin(ang)
    one = jnp.ones((n_pos, LANES - 2 * half), F32)
    zero = jnp.zeros((n_pos, LANES - 2 * half), F32)
    zh = jnp.zeros((n_pos, half), F32)
    c = jnp.concatenate([cos, cos, one], axis=1)
    sa = jnp.concatenate([-sin, zh, zero], axis=1)
    sb = jnp.concatenate([zh, sin, zero], axis=1)
    return c, sa, sb


def _rope_group(x, c, sa, sb, half):
    return (x * c + pltpu.roll(x, LANES - half, axis=1) * sa + pltpu.roll(x, half, axis=1) * sb)


CONV_HALO = 16


def _conv_kernel(main_ref, prev_ref, next_ref, w_ref, b_ref, o_ref, buf_ref, *, layout, tq):
    i = pl.program_id(0)
    pos, seqlen = layout.pos_and_len(i * tq)
    first = pos == 0
    last = pos + tq == seqlen
    buf_ref[0:CONV_HALO, :] = jnp.where(first, 0.0, prev_ref[...].astype(F32))
    buf_ref[CONV_HALO:CONV_HALO + tq, :] = main_ref[...].astype(F32)
    buf_ref[CONV_HALO + tq:, :] = jnp.where(last, 0.0, next_ref[...].astype(F32))
    acc = jnp.zeros(o_ref.shape, F32) + b_ref[...]
    for k in range(CONV_K):
        start = CONV_HALO + k - CONV_K // 2
        acc = acc + buf_ref[start:start + tq, :] * w_ref[k:k + 1, :]
    o_ref[...] = (acc * (1.0 / (1.0 + jnp.exp(-acc)))).astype(o_ref.dtype)


def conv_silu(zx, col0, conv_w, conv_b, layout, tq=256, tc=512):
    t = zx.shape[0]
    c = conv_w.shape[1]
    tq = _pick(min(layout.l0, layout.l1), tq)
    cb0 = col0 // tc
    hb = tq // CONV_HALO
    nh = t // CONV_HALO
    return pl.pallas_call(
        functools.partial(_conv_kernel, layout=layout, tq=tq),
        grid=(t // tq, c // tc),
        in_specs=[pl.BlockSpec((tq, tc), lambda i, j: (i, cb0 + j)),
                  pl.BlockSpec((CONV_HALO, tc), lambda i, j: (jnp.maximum(i * hb - 1, 0), cb0 + j)),
                  pl.BlockSpec((CONV_HALO, tc), lambda i, j: (jnp.minimum((i + 1) * hb, nh - 1), cb0 + j)),
                  pl.BlockSpec((CONV_K, tc), lambda i, j: (0, j)),
                  pl.BlockSpec((1, tc), lambda i, j: (0, j))],
        out_specs=pl.BlockSpec((tq, tc), lambda i, j: (i, j)),
        out_shape=jax.ShapeDtypeStruct((t, c), BF16),
        scratch_shapes=[pltpu.VMEM((tq + 2 * CONV_HALO, tc), F32)],
        compiler_params=_cparams(("parallel", "parallel")),
        name="conv_silu",
    )(zx, zx, zx, conv_w.astype(F32), conv_b.reshape(1, c).astype(F32))


def _split3(x):
    hi = x.astype(BF16)
    r1 = x - hi.astype(F32)
    mid = r1.astype(BF16)
    lo = (r1 - mid.astype(F32)).astype(BF16)
    return hi, mid, lo


def _expand_heads(v, e_ref):
    hi, mid, lo = _split3(v)
    e = e_ref[...]
    return (jnp.dot(hi, e, preferred_element_type=F32) + jnp.dot(mid, e, preferred_element_type=F32)
            + jnp.dot(lo, e, preferred_element_type=F32))


def _softplus(x):
    return jnp.maximum(x, 0.0) + jnp.log1p(jnp.exp(-jnp.abs(x)))


def _ssd_kernel(x_ref, b_ref, c_ref, dt_ref, dtt_ref, bias_ref, biast_ref, alog_ref, alogt_ref,
                e_ref, y_ref, s_ref, *, layout, rev):
    q = SSD_CHUNK
    hpg = SSD_HEADS // SSD_GROUPS
    i = pl.program_id(0)
    nchunks = pl.num_programs(0)
    chunk = (nchunks - 1 - i) if rev else i
    pos, seqlen = layout.pos_and_len(chunk * q)
    starts = (pos + q == seqlen) if rev else (pos == 0)

    @pl.when(starts)
    def _():
        s_ref[...] = jnp.zeros_like(s_ref)

    h0 = SSD_HEADS if rev else 0
    dt = _softplus(dt_ref[...] + bias_ref[...])
    da = dt * (-jnp.exp(alog_ref[...]))
    dt_t = _softplus(dtt_ref[...] + biast_ref[...])
    da_t = dt_t * (-jnp.exp(alogt_ref[...]))

    row = lax.broadcasted_iota(jnp.int32, (q, q), 0)
    col = lax.broadcasted_iota(jnp.int32, (q, q), 1)
    mask = (col >= row) if rev else (col <= row)
    tri = jnp.where(mask, 1.0, 0.0).astype(BF16)
    tri_t = jnp.where((row >= col) if rev else (row <= col), 1.0, 0.0).astype(BF16)

    hi, mid, lo = _split3(da)
    cum = (jnp.dot(tri, hi, preferred_element_type=F32) + jnp.dot(tri, mid, preferred_element_type=F32)
           + jnp.dot(tri, lo, preferred_element_type=F32))
    hi, mid, lo = _split3(da_t)
    cum_t = (jnp.dot(hi, tri_t, preferred_element_type=F32) + jnp.dot(mid, tri_t, preferred_element_type=F32)
             + jnp.dot(lo, tri_t, preferred_element_type=F32))

    end = 0 if rev else q - 1
    cum_end = cum[end:end + 1, :]
    ecum = jnp.exp(cum)
    wgt = dt * jnp.exp(cum_end - cum)

    xb = x_ref[...]
    bm = b_ref[...]
    cm = c_ref[...]
    s_in = s_ref[...].astype(BF16)

    for g in range(SSD_GROUPS):
        bg = bm[:, g * SSD_STATE:(g + 1) * SSD_STATE]
        cg = cm[:, g * SSD_STATE:(g + 1) * SSD_STATE]
        cb = lax.dot_general(cg, bg, (((1,), (1,)), ((), ())), preferred_element_type=F32)
        cg32 = cg.astype(F32)
        for pair in range(hpg // 2):
            lane0 = (g * hpg + 2 * pair) * SSD_HEAD_DIM
            rhs = jnp.concatenate([xb[:, lane0:lane0 + LANES], s_in[:, lane0:lane0 + LANES]], axis=0)
            res = []
            for sub in range(2):
                h = h0 + g * hpg + 2 * pair + sub
                diff = cum[:, h:h + 1] - cum_t[h:h + 1, :]
                dec = jnp.exp(jnp.where(mask, diff, NEG_BIG))
                m = (cb * dec * dt_t[h:h + 1, :]).astype(BF16)
                ce = (cg32 * ecum[:, h:h + 1]).astype(BF16)
                lhs = jnp.concatenate([m, ce], axis=1)
                res.append(jnp.dot(lhs, rhs, preferred_element_type=F32))
            lane = lax.broadcasted_iota(jnp.int32, (q, LANES), 1)
            y_ref[:, lane0:lane0 + LANES] = jnp.where(lane < SSD_HEAD_DIM, res[0], res[1])

    xw = (xb.astype(F32) * _expand_heads(wgt, e_ref)).astype(BF16)
    decay8 = _expand_heads(jnp.exp(cum[end - end % 8:end - end % 8 + 8, :]), e_ref)
    decay = decay8[end % 8:end % 8 + 1, :]
    gw = SSD_HEADS // SSD_GROUPS * SSD_HEAD_DIM
    for g in range(SSD_GROUPS):
        bg_t = bm[:, g * SSD_STATE:(g + 1) * SSD_STATE].astype(F32).T.astype(BF16)
        upd = jnp.dot(bg_t, xw[:, g * gw:(g + 1) * gw], preferred_element_type=F32)
        s_ref[:, g * gw:(g + 1) * gw] = s_ref[:, g * gw:(g + 1) * gw] * decay[:, g * gw:(g + 1) * gw] + upd


def ssd_scan(xbc, small, dt_col0, dt_t, dt_bias, a_log, layout, rev):
    t = xbc.shape[0]
    q = SSD_CHUNK
    n = t // q
    ci = (lambda i: n - 1 - i) if rev else (lambda i: i)
    bc_w = SSD_GROUPS * SSD_STATE
    h0 = SSD_HEADS if rev else 0
    expand = jnp.zeros((LANES, D_SSM), BF16).at[h0:h0 + SSD_HEADS].set(
        jnp.repeat(jnp.eye(SSD_HEADS, dtype=BF16), SSD_HEAD_DIM, axis=1))
    full = lambda a: pl.BlockSpec(a.shape, lambda i: (0,) * a.ndim)
    pad = jnp.zeros((LANES - 2 * SSD_HEADS,), F32)
    bias = jnp.concatenate([dt_bias.astype(F32).reshape(-1), pad]).reshape(1, LANES)
    alog = jnp.concatenate([a_log.astype(F32).reshape(-1), pad]).reshape(1, LANES)
    bias_t = dt_bias.astype(F32).reshape(2 * SSD_HEADS, 1)
    alog_t = a_log.astype(F32).reshape(2 * SSD_HEADS, 1)
    args = (xbc, xbc, xbc, small, dt_t, bias, bias_t, alog, alog_t, expand)
    return pl.pallas_call(
        functools.partial(_ssd_kernel, layout=layout, rev=rev),
        grid=(n,),
        in_specs=[pl.BlockSpec((q, D_SSM), lambda i: (ci(i), 0)),
                  pl.BlockSpec((q, bc_w), lambda i: (ci(i), D_SSM // bc_w)),
                  pl.BlockSpec((q, bc_w), lambda i: (ci(i), D_SSM // bc_w + 1)),
                  pl.BlockSpec((q, LANES), lambda i: (ci(i), dt_col0 // LANES)),
                  pl.BlockSpec((2 * SSD_HEADS, q), lambda i: (0, ci(i))),
                  full(bias), full(bias_t), full(alog), full(alog_t), full(expand)],
        out_specs=pl.BlockSpec((q, D_SSM), lambda i: (ci(i), 0)),
        out_shape=jax.ShapeDtypeStruct((t, D_SSM), F32),
        scratch_shapes=[pltpu.VMEM((SSD_STATE, D_SSM), F32)],
        compiler_params=_cparams(("arbitrary",)),
        name="ssd_scan_bwd" if rev else "ssd_scan_fwd",
    )(*args)


def _ssd_combine_kernel(yf_ref, yb_ref, x_ref, z_ref, dskip_ref, g_ref, o_ref):
    x = x_ref[...].astype(F32)
    z = z_ref[...].astype(F32)
    y = (yf_ref[...] + yb_ref[...] + x * dskip_ref[...]) * (z * (1.0 / (1.0 + jnp.exp(-z))))
    gw = D_SSM // SSD_GROUPS
    for g in range(SSD_GROUPS):
        sl = slice(g * gw, (g + 1) * gw)
        o_ref[:, sl] = _rms(y[:, sl], g_ref[:, sl]).astype(o_ref.dtype)


def ssd_combine(y_f, y_b, xbc, zx, d_skip, norm_w, tm=512):
    t = y_f.shape[0]
    tm = _pick(t, tm)
    dvec = jnp.repeat(d_skip.astype(F32), SSD_HEAD_DIM).reshape(1, D_SSM)
    row = pl.BlockSpec((tm, D_SSM), lambda i: (i, 0))
    vec = pl.BlockSpec((1, D_SSM), lambda i: (0, 0))
    return pl.pallas_call(
        _ssd_combine_kernel,
        grid=(t // tm,),
        in_specs=[row, row, row, row, vec, vec],
        out_specs=row,
        out_shape=jax.ShapeDtypeStruct((t, D_SSM), BF16),
        compiler_params=_cparams(("parallel",)),
        name="ssd_combine",
    )(y_f, y_b, xbc, zx, dvec, norm_w.reshape(1, D_SSM).astype(F32))


def _mla_prep_kernel(cq_ref, ckv_ref, kpe_ref, gq_ref, gkv_ref, wq_ref, wkv_ref, c_ref, sa_ref, sb_ref,
                     q_ref, k_ref, v_ref, *, scale):
    half = QK_ROPE // 2
    c, sa, sb = c_ref[...], sa_ref[...], sb_ref[...]
    nw = MLA_HEADS * QK_NOPE
    qn = _rms(cq_ref[...], gq_ref[...]).astype(BF16)
    q = jnp.dot(qn, wq_ref[...], preferred_element_type=F32) * scale
    kvn = _rms(ckv_ref[...], gkv_ref[...]).astype(BF16)
    kv = jnp.dot(kvn, wkv_ref[...], preferred_element_type=F32)
    kpe = _rope_group(kpe_ref[...], c, sa, sb, half).astype(BF16)
    for h in range(MLA_HEADS):
        lo = h * 2 * LANES
        q_ref[:, lo:lo + LANES] = q[:, h * LANES:(h + 1) * LANES].astype(BF16)
        qpe = q[:, nw + h * LANES:nw + (h + 1) * LANES]
        q_ref[:, lo + LANES:lo + 2 * LANES] = _rope_group(qpe, c, sa, sb, half).astype(BF16)
        k_ref[:, lo:lo + LANES] = kv[:, h * LANES:(h + 1) * LANES].astype(BF16)
        k_ref[:, lo + LANES:lo + 2 * LANES] = kpe
    v_ref[...] = kv[:, nw:].astype(BF16)


def mla_prep(small, gq, gkv, wq, wkv, tabs, layout, tm=256):
    t = small.shape[0]
    tm = _pick(min(layout.l0, layout.l1), tm)
    hq = MLA_HEADS * 2 * LANES
    tab = pl.BlockSpec((tm, LANES), lambda i: (layout.table_block(i, tm), 0))
    full = lambda a: pl.BlockSpec(a.shape, lambda i: (0,) * a.ndim)
    gq2 = gq.reshape(1, Q_LORA).astype(F32)
    gkv2 = gkv.reshape(1, KV_LORA).astype(F32)
    return pl.pallas_call(
        functools.partial(_mla_prep_kernel, scale=LOG2_E / math.sqrt(QK_NOPE + QK_ROPE)),
        grid=(t // tm,),
        in_specs=[pl.BlockSpec((tm, Q_LORA), lambda i: (i, 0)),
                  pl.BlockSpec((tm, KV_LORA), lambda i: (i, 1)),
                  pl.BlockSpec((tm, LANES), lambda i: (i, (Q_LORA + KV_LORA) // LANES)),
                  full(gq2), full(gkv2), full(wq), full(wkv), tab, tab, tab],
        out_specs=[pl.BlockSpec((tm, hq), lambda i: (i, 0)),
                   pl.BlockSpec((tm, hq), lambda i: (i, 0)),
                   pl.BlockSpec((tm, MLA_HEADS * V_HEAD), lambda i: (i, 0))],
        out_shape=[jax.ShapeDtypeStruct((t, hq), BF16),
                   jax.ShapeDtypeStruct((t, hq), BF16),
                   jax.ShapeDtypeStruct((t, MLA_HEADS * V_HEAD), BF16)],
        compiler_params=_cparams(("parallel",)),
        name="mla_prep",
    )(small, small, small, gq2, gkv2, wq, wkv, *tabs)


def _rope_cols_kernel(x_ref, c_ref, sa_ref, sb_ref, o_ref, *, half, scale_groups, scale):
    c, sa, sb = c_ref[...], sa_ref[...], sb_ref[...]
    j = pl.program_id(1)
    n = x_ref.shape[1] // LANES
    f = jnp.where(j * n < scale_groups, scale, 1.0)
    for gidx in range(n):
        sl = slice(gidx * LANES, (gidx + 1) * LANES)
        o_ref[:, sl] = (_rope_group(x_ref[:, sl], c, sa, sb, half) * f).astype(o_ref.dtype)


def rope_cols(x, ncols, tabs, half, scale_groups, scale, layout, tm=256, tc=1024):
    t = x.shape[0]
    tm = _pick(min(layout.l0, layout.l1), tm)
    tc = _pick(ncols, tc)
    assert scale_groups % (tc // LANES) == 0
    tab = pl.BlockSpec((tm, LANES), lambda i, j: (layout.table_block(i, tm), 0))
    return pl.pallas_call(
        functools.partial(_rope_cols_kernel, half=half, scale_groups=scale_groups, scale=scale),
        grid=(t // tm, ncols // tc),
        in_specs=[pl.BlockSpec((tm, tc), lambda i, j: (i, j)), tab, tab, tab],
        out_specs=pl.BlockSpec((tm, tc), lambda i, j: (i, j)),
        out_shape=jax.ShapeDtypeStruct((t, ncols), BF16),
        compiler_params=_cparams(("parallel", "parallel")),
        name="rope_cols",
    )(x, *tabs)


def _reduce_rows(x, op):
    rows = x.shape[0]
    while rows > 8 and rows % 8 == 0:
        x = op(x.reshape(8, rows // 8, x.shape[1]), axis=0)
        rows //= 8
    return op(x, axis=0, keepdims=True)


def _flash_kernel(*refs, n_maps, seq_len, bk, lam_init):
    if n_maps == 2:
        q_ref, k_ref, vt_ref, lam_ref, subln_ref, o_ref, m_ref, l_ref, acc_ref, sa_ref, sb_ref = refs
    else:
        q_ref, k_ref, vt_ref, o_ref, m_ref, l_ref, acc_ref, sa_ref, sb_ref = refs
    dqk = q_ref.shape[1] // n_maps
    n_chunks = seq_len // bk
    m_ref[...] = jnp.full_like(m_ref, NEG_BIG)
    l_ref[...] = jnp.zeros_like(l_ref)
    acc_ref[...] = jnp.zeros_like(acc_ref)
    qs = [q_ref[:, t * dqk:(t + 1) * dqk] for t in range(n_maps)]

    def scores(j, t):
        off = pl.multiple_of(j * bk, bk)
        return lax.dot_general(k_ref[pl.ds(off, bk), t * dqk:(t + 1) * dqk], qs[t],
                               (((1,), (1,)), ((), ())), preferred_element_type=F32)

    def accumulate(j, t, s):
        off = pl.multiple_of(j * bk, bk)
        m_old = m_ref[t]
        m_new = jnp.maximum(m_old, _reduce_rows(s, jnp.max))
        alpha = jnp.exp2(m_old - m_new)
        p = jnp.exp2(s - m_new)
        l_ref[t] = alpha * l_ref[t] + _reduce_rows(p, jnp.sum)
        m_ref[t] = m_new
        pv = jnp.dot(vt_ref[:, pl.ds(off, bk)], p.astype(BF16), preferred_element_type=F32)
        acc_ref[t] = alpha * acc_ref[t] + pv

    def step(j, cur_ref, next_ref, has_next):
        for t in range(n_maps):
            if has_next:
                next_ref[t] = scores(j + 1, t)
            accumulate(j, t, cur_ref[t])

    for t in range(n_maps):
        sa_ref[t] = scores(0, t)
    n_pairs = (n_chunks - 1) // 2

    def body(i, carry):
        step(2 * i, sa_ref, sb_ref, True)
        step(2 * i + 1, sb_ref, sa_ref, True)
        return carry

    lax.fori_loop(0, n_pairs, body, 0)
    bufs = (sa_ref, sb_ref)
    for j in range(2 * n_pairs, n_chunks):
        step(j, bufs[0], bufs[1], j + 1 < n_chunks)
        bufs = (bufs[1], bufs[0])

    if n_maps == 1:
        o_ref[...] = (acc_ref[0] / l_ref[0]).T.astype(o_ref.dtype)
    else:
        lp = lam_ref[...]
        lam = (jnp.exp(jnp.sum(lp[0:1] * lp[1:2], axis=-1, keepdims=True))
               - jnp.exp(jnp.sum(lp[2:3] * lp[3:4], axis=-1, keepdims=True)) + lam_init)
        o = (acc_ref[0] / l_ref[0] - lam * (acc_ref[1] / l_ref[1])).T
        o_ref[...] = (_rms(o, subln_ref[...]) * (1.0 - lam_init)).astype(o_ref.dtype)


def flash_attention(q, k, vt, n_heads, n_maps, q_col0, k_col0, v_row0, row0, n_seq, seq_len,
                    lam=None, subln=None, lam_init=0.0, bq=512, bk=512):
    hq = 2 * LANES
    dv = LANES * n_maps
    bq = _pick(seq_len, bq)
    bk = _pick(seq_len, bk)
    nq = seq_len // bq
    assert row0 % seq_len == 0 and q_col0 % hq == 0 and k_col0 % hq == 0 and v_row0 % dv == 0
    qb0, sb0 = row0 // bq, row0 // seq_len
    in_specs = [pl.BlockSpec((bq, hq), lambda b, h, i: (qb0 + b * nq + i, q_col0 // hq + h)),
                pl.BlockSpec((seq_len, hq), lambda b, h, i: (sb0 + b, k_col0 // hq + h)),
                pl.BlockSpec((dv, seq_len), lambda b, h, i: (v_row0 // dv + h, sb0 + b))]
    args = [q, k, vt]
    if n_maps == 2:
        in_specs += [pl.BlockSpec(lam.shape, lambda b, h, i: (0, 0)),
                     pl.BlockSpec(subln.shape, lambda b, h, i: (0, 0))]
        args += [lam, subln]
    return pl.pallas_call(
        functools.partial(_flash_kernel, n_maps=n_maps, seq_len=seq_len, bk=bk, lam_init=lam_init),
        grid=(n_seq, n_heads, nq),
        in_specs=in_specs,
        out_specs=pl.BlockSpec((bq, dv), lambda b, h, i: (b * nq + i, h)),
        out_shape=jax.ShapeDtypeStruct((n_seq * seq_len, n_heads * dv), BF16),
        scratch_shapes=[pltpu.VMEM((n_maps, 1, bq), F32), pltpu.VMEM((n_maps, 1, bq), F32),
                        pltpu.VMEM((n_maps, dv, bq), F32),
                        pltpu.VMEM((n_maps, bk, bq), F32), pltpu.VMEM((n_maps, bk, bq), F32)],
        compiler_params=_cparams(("parallel", "parallel", "parallel")),
        name="diff_attention" if n_maps == 2 else "mla_attention",
    )(*args)


def attention_all(layout, **kw):
    parts = []
    if layout.n0:
        parts.append(flash_attention(row0=0, n_seq=layout.n0, seq_len=layout.l0, **kw))
    if layout.n1:
        parts.append(flash_attention(row0=layout.r0, n_seq=layout.n1, seq_len=layout.l1, **kw))
    return jnp.concatenate(parts, axis=0)


def _even_in_weight(w):
    o1 = D_SSM
    o2 = o1 + CONV_DIM
    o3 = o2 + 2 * SSD_HEADS
    o4 = o3 + Q_LORA
    o5 = o4 + KV_LORA
    d = w.shape[0]
    zeros = lambda n: jnp.zeros((d, n), w.dtype)
    main = w[:, :o2]
    small = jnp.concatenate([w[:, o3:o4], w[:, o4:o5], w[:, o5:], zeros(LANES - QK_ROPE),
                             w[:, o2:o3], zeros(LANES - 2 * SSD_HEADS)], axis=1)
    return main.astype(BF16), small.astype(BF16)


def _mla_up_weights(w_q_up, w_kv_up):
    wq = w_q_up.reshape(Q_LORA, MLA_HEADS, QK_NOPE + QK_ROPE)
    q_nope = wq[:, :, :QK_NOPE].reshape(Q_LORA, MLA_HEADS * QK_NOPE)
    q_pe = jnp.pad(wq[:, :, QK_NOPE:], ((0, 0), (0, 0), (0, LANES - QK_ROPE))).reshape(Q_LORA, MLA_HEADS * LANES)
    wkv = w_kv_up.reshape(KV_LORA, MLA_HEADS, QK_NOPE + V_HEAD)
    k_nope = wkv[:, :, :QK_NOPE].reshape(KV_LORA, MLA_HEADS * QK_NOPE)
    v = wkv[:, :, QK_NOPE:].reshape(KV_LORA, MLA_HEADS * V_HEAD)
    return (jnp.concatenate([q_nope, q_pe], axis=1).astype(BF16),
            jnp.concatenate([k_nope, v], axis=1).astype(BF16))


def _even_mixer(u, p, i, layout, tabs):
    w_main, w_small = _even_in_weight(p['w_in_even'][i])
    zx = matmul(u, w_main, BF16)
    small = matmul(u, w_small, F32)
    dt_col0 = Q_LORA + KV_LORA + LANES

    xbc = conv_silu(zx, D_SSM, p['conv_w'][i], p['conv_b'][i], layout)
    dt_t = small[:, dt_col0:dt_col0 + 2 * SSD_HEADS].T
    y_f = ssd_scan(xbc, small, dt_col0, dt_t, p['dt_bias'][i], p['a_log'][i], layout, rev=False)
    y_b = ssd_scan(xbc, small, dt_col0, dt_t, p['dt_bias'][i], p['a_log'][i], layout, rev=True)
    y = ssd_combine(y_f, y_b, xbc, zx, p['d_skip'][i], p['ssm_norm_w'][i])

    wq, wkv = _mla_up_weights(p['w_q_up'][i], p['w_kv_up'][i])
    q, k, v = mla_prep(small, p['q_norm_w'][i], p['kv_norm_w'][i], wq, wkv, tabs, layout)
    o = attention_all(layout, q=q, k=k, vt=v.T, n_heads=MLA_HEADS, n_maps=1, q_col0=0, k_col0=0, v_row0=0)
    w_out = p['w_out_even'][i].astype(BF16)
    return (y, o), (w_out[:D_SSM], w_out[D_SSM:])


def _diff_mixer(u, p, j, layer_idx, layout, tabs):
    d_diff = DIFF_HEADS * 2 * DIFF_HEAD_DIM
    w_in = p['w_in_odd'][j].astype(BF16)
    qk = matmul(u, w_in[:, :2 * d_diff], F32)
    v = matmul(u, w_in[:, 2 * d_diff:], BF16)
    qk = rope_cols(qk, 2 * d_diff, tabs, DIFF_ROT // 2, d_diff // LANES, LOG2_E / math.sqrt(DIFF_HEAD_DIM), layout)
    lam_init = 0.8 - 0.6 * math.exp(-0.3 * layer_idx)
    o = attention_all(layout, q=qk, k=qk, vt=v.T, n_heads=DIFF_HEADS, n_maps=2, q_col0=0, k_col0=d_diff,
                      v_row0=0, lam=p['diff_lambda'][j].astype(F32),
                      subln=p['subln_w'][j].reshape(1, -1).astype(F32), lam_init=lam_init)
    return (o,), (p['w_out_odd'][j].astype(BF16),)


def kernel(x_prompt, x_sample, norm_pre_mix, norm_post_mix, norm_pre_mlp, norm_post_mlp,
           w_in_even, conv_w, conv_b, dt_bias, a_log, d_skip, ssm_norm_w, q_norm_w, w_q_up,
           kv_norm_w, w_kv_up, w_out_even, w_in_odd, diff_lambda, subln_w, w_out_odd,
           w_mlp_up, w_mlp_down):
    p = dict(w_in_even=w_in_even, conv_w=conv_w, conv_b=conv_b, dt_bias=dt_bias, a_log=a_log,
             d_skip=d_skip, ssm_norm_w=ssm_norm_w, q_norm_w=q_norm_w, w_q_up=w_q_up,
             kv_norm_w=kv_norm_w, w_kv_up=w_kv_up, w_out_even=w_out_even, w_in_odd=w_in_odd,
             diff_lambda=diff_lambda, subln_w=subln_w, w_out_odd=w_out_odd)
    n0, l0, d = x_prompt.shape
    n1, l1, _ = x_sample.shape
    layout = SeqLayout(n0, l0, n1, l1)
    depth = norm_pre_mix.shape[0]
    n_pos = max(l0, l1)
    tabs_mla = rope_tables(QK_ROPE // 2, n_pos)
    tabs_diff = rope_tables(DIFF_ROT // 2, n_pos)

    h = jnp.concatenate([x_prompt.reshape(n0 * l0, d), x_sample.reshape(n1 * l1, d)], axis=0)
    for i in range(depth):
        u = rmsnorm(h, norm_pre_mix[i])
        if i % 2 == 0:
            acts, ws = _even_mixer(u, p, i // 2, layout, tabs_mla)
        else:
            acts, ws = _diff_mixer(u, p, i // 2, i, layout, tabs_diff)
        h = outproj_postnorm_residual(acts, ws, norm_post_mix[i], h)
        h = mlp_block(h, norm_pre_mlp[i], w_mlp_up[i].astype(BF16), w_mlp_down[i].astype(BF16),
                      norm_post_mlp[i])
    return (h[:layout.r0].reshape(n0, l0, d), h[layout.r0:].reshape(n1, l1, d))
```

```python
import functools
import math

import jax
import jax.numpy as jnp
from jax import lax
from jax.experimental import pallas as pl
from jax.experimental.pallas import tpu as pltpu

F32 = jnp.float32
BF16 = jnp.bfloat16

EPS = 1e-6
ROPE_THETA = 500000.0
LANES = 128
VMEM_LIMIT_BYTES = 56 * 1024 * 1024

SSD_HEADS = 16
SSD_HEAD_DIM = 64
SSD_GROUPS = 2
SSD_STATE = 128
D_SSM = SSD_HEADS * SSD_HEAD_DIM
CONV_K = 5
CONV_DIM = D_SSM + 2 * SSD_GROUPS * SSD_STATE
MLA_HEADS = 8
QK_NOPE = 128
QK_ROPE = 64
V_HEAD = 128
Q_LORA = 512
KV_LORA = 512
DIFF_HEADS = 8
DIFF_HEAD_DIM = 128
DIFF_ROT = DIFF_HEAD_DIM // 4
SSD_CHUNK = 128
NEG_BIG = -1e30
LOG2_E = math.log2(math.e)
ONES_ROWS = 16


def _cparams(sem):
    return pltpu.CompilerParams(dimension_semantics=sem, vmem_limit_bytes=VMEM_LIMIT_BYTES)


def _pick(n, pref):
    t = min(n, pref)
    while n % t:
        t //= 2
    return t


def _rms(x, g):
    return x * lax.rsqrt(jnp.mean(x * x, axis=-1, keepdims=True) + EPS) * g


class Seg:
    def __init__(self, j0, j1, kind, out, scale=1.0):
        self.j0, self.j1, self.kind, self.out, self.scale = j0, j1, kind, out, scale


def _proj_kernel(*refs, segs, n_out, has_wt, has_rope):
    refs = list(refs)
    x_ref, g_ref, w_ref = refs[:3]
    pos = 3
    wt_ref = None
    if has_wt:
        wt_ref = refs[pos]
        pos += 1
    if has_rope:
        c_ref, s_ref = refs[pos:pos + 2]
        pos += 2
    out_refs = refs[pos:pos + n_out]
    xn_ref = refs[pos + n_out]
    j = pl.program_id(1)

    @pl.when(j == 0)
    def _():
        xn_ref[...] = _rms(x_ref[...], g_ref[...]).astype(BF16)

    for seg in segs:
        o_ref = out_refs[seg.out]

        @pl.when((j >= seg.j0) & (j < seg.j1))
        def _(seg=seg, o_ref=o_ref):
            if seg.kind == 'trans':
                o_ref[...] = lax.dot_general(wt_ref[...], xn_ref[...], (((1,), (1,)), ((), ())),
                                             preferred_element_type=F32).astype(o_ref.dtype)
                return
            acc = jnp.dot(xn_ref[...], w_ref[...], preferred_element_type=F32)
            if seg.kind == 'plain':
                o_ref[...] = acc.astype(o_ref.dtype)
            else:
                c, s = c_ref[...] * seg.scale, s_ref[...] * seg.scale
                for gi in range(acc.shape[1] // LANES):
                    sl = slice(gi * LANES, (gi + 1) * LANES)
                    o_ref[:, sl] = _rope_group(acc[:, sl], c, s).astype(o_ref.dtype)


def norm_proj(h, g, w, segs, outs, tn, layout, wt=None, tabs=None, tm=1024):
    t, d = h.shape
    tm = _pick(min(layout.l0, layout.l1, tm), tm)
    n_tiles = segs[-1].j1
    has_wt, has_rope = wt is not None, tabs is not None
    w_tiles = w.shape[1] // tn
    in_specs = [pl.BlockSpec((tm, d), lambda i, j: (i, 0)),
                pl.BlockSpec((1, d), lambda i, j: (0, 0)),
                pl.BlockSpec((d, tn), lambda i, j: (0, jnp.minimum(j, w_tiles - 1)))]
    args = [h, g.reshape(1, d).astype(F32), w]
    if has_wt:
        in_specs.append(pl.BlockSpec((tn, d), lambda i, j: (jnp.maximum(j - w_tiles, 0), 0)))
        args.append(wt)
    if has_rope:
        tab = pl.BlockSpec((tm, LANES), lambda i, j: (layout.table_block(i, tm), 0))
        in_specs += [tab, tab]
        args += list(tabs)
    out_specs, out_shapes = [], []
    for k, (shape, dtype) in enumerate(outs):
        mine = [s for s in segs if s.out == k]
        j0, j1 = mine[0].j0, mine[-1].j1
        if mine[0].kind == 'trans':
            out_specs.append(pl.BlockSpec((tn, tm), lambda i, j, j0=j0, j1=j1: (jnp.clip(j - j0, 0, j1 - j0 - 1), i)))
        else:
            out_specs.append(pl.BlockSpec((tm, tn), lambda i, j, j0=j0, j1=j1: (i, jnp.clip(j - j0, 0, j1 - j0 - 1))))
        out_shapes.append(jax.ShapeDtypeStruct(shape, dtype))
    return pl.pallas_call(
        functools.partial(_proj_kernel, segs=segs, n_out=len(outs), has_wt=has_wt, has_rope=has_rope),
        grid=(t // tm, n_tiles),
        in_specs=in_specs,
        out_specs=out_specs,
        out_shape=out_shapes,
        scratch_shapes=[pltpu.VMEM((tm, d), BF16)],
        compiler_params=_cparams(("parallel", "arbitrary")),
        name="norm_proj",
    )(*args)


def _outproj_kernel(*refs, n_in):
    a_refs = refs[:n_in]
    w_refs = refs[n_in:2 * n_in]
    g_ref, h_ref, o_ref = refs[2 * n_in:]
    acc = jnp.dot(a_refs[0][...], w_refs[0][...], preferred_element_type=F32)
    for a_ref, w_ref in zip(a_refs[1:], w_refs[1:]):
        acc = acc + jnp.dot(a_ref[...], w_ref[...], preferred_element_type=F32)
    o_ref[...] = h_ref[...] + _rms(acc, g_ref[...])


def outproj_postnorm_residual(acts, ws, g, h, tm=512):
    t, d = h.shape
    tm = _pick(t, tm)
    n_in = len(acts)
    in_specs = ([pl.BlockSpec((tm, a.shape[1]), lambda i: (i, 0)) for a in acts]
                + [pl.BlockSpec(w.shape, lambda i: (0, 0)) for w in ws]
                + [pl.BlockSpec((1, d), lambda i: (0, 0)),
                   pl.BlockSpec((tm, d), lambda i: (i, 0))])
    return pl.pallas_call(
        functools.partial(_outproj_kernel, n_in=n_in),
        grid=(t // tm,),
        in_specs=in_specs,
        out_specs=pl.BlockSpec((tm, d), lambda i: (i, 0)),
        out_shape=jax.ShapeDtypeStruct((t, d), F32),
        compiler_params=_cparams(("parallel",)),
        name="outproj_postnorm_residual",
    )(*acts, *ws, g.reshape(1, d).astype(F32), h)


def _mlp_kernel(h_ref, gpre_ref, wup_ref, wdn_ref, gpost_ref, *rest, block_starts):
    out_refs = rest[:len(block_starts)]
    xn_ref, acc_ref = rest[len(block_starts):]
    i = pl.program_id(0)
    j = pl.program_id(1)

    @pl.when(j == 0)
    def _():
        xn_ref[...] = _rms(h_ref[...], gpre_ref[...]).astype(BF16)
        acc_ref[...] = jnp.zeros_like(acc_ref)

    a = jnp.dot(xn_ref[...], wup_ref[...], preferred_element_type=F32)
    a = jnp.square(jnp.maximum(a, 0.0)).astype(BF16)
    acc_ref[...] += jnp.dot(a, wdn_ref[...], preferred_element_type=F32)

    ends = list(block_starts[1:]) + [pl.num_programs(0)]
    for o_ref, start, end in zip(out_refs, block_starts, ends):
        @pl.when((j == pl.num_programs(1) - 1) & (i >= start) & (i < end))
        def _(o_ref=o_ref):
            o_ref[...] = h_ref[...] + _rms(acc_ref[...], gpost_ref[...])


def mlp_block(h, g_pre, w_up, w_down, g_post, out_rows=None, tm=512, tf=512):
    t, d = h.shape
    f = w_up.shape[1]
    out_rows = (t,) if out_rows is None else tuple(r for r in out_rows if r)
    tm = _pick(min(out_rows), tm)
    tf = _pick(f, tf)
    block_starts, out_specs = [], []
    start = 0
    for r in out_rows:
        nb = r // tm
        out_specs.append(pl.BlockSpec((tm, d), lambda i, j, start=start, nb=nb: (jnp.clip(i - start, 0, nb - 1), 0)))
        block_starts.append(start)
        start += nb
    outs = pl.pallas_call(
        functools.partial(_mlp_kernel, block_starts=tuple(block_starts)),
        grid=(t // tm, f // tf),
        in_specs=[pl.BlockSpec((tm, d), lambda i, j: (i, 0)),
                  pl.BlockSpec((1, d), lambda i, j: (0, 0)),
                  pl.BlockSpec((d, tf), lambda i, j: (0, j)),
                  pl.BlockSpec((tf, d), lambda i, j: (j, 0)),
                  pl.BlockSpec((1, d), lambda i, j: (0, 0))],
        out_specs=out_specs,
        out_shape=[jax.ShapeDtypeStruct((r, d), F32) for r in out_rows],
        scratch_shapes=[pltpu.VMEM((tm, d), BF16), pltpu.VMEM((tm, d), F32)],
        compiler_params=_cparams(("arbitrary", "arbitrary")),
        name="mlp_block",
    )(h, g_pre.reshape(1, d).astype(F32), w_up, w_down, g_post.reshape(1, d).astype(F32))
    return outs


class SeqLayout:
    def __init__(self, n0, l0, n1, l1):
        self.n0, self.l0, self.n1, self.l1 = n0, l0, n1, l1
        self.r0 = n0 * l0
        self.t = self.r0 + n1 * l1
        assert self.r0 % l1 == 0, "sample sequences must start on a multiple of their length"

    def pos_and_len(self, row):
        in0 = row < self.r0
        pos = jnp.where(in0, lax.rem(row, self.l0), lax.rem(row - self.r0, self.l1))
        return pos, jnp.where(in0, self.l0, self.l1)

    def table_block(self, i, tm):
        b0 = self.r0 // tm
        return jnp.where(i < b0, lax.rem(i, self.l0 // tm), lax.rem(i - b0, self.l1 // tm))


HALF_LANES = LANES // 2


def rope_tables(half, n_pos):
    inv_freq = ROPE_THETA ** (-jnp.arange(half, dtype=F32) / half)
    ang = jnp.arange(n_pos, dtype=F32)[:, None] * inv_freq[None, :]
    cos, sin = jnp.cos(ang), jnp.sin(ang)
    one = jnp.ones((n_pos, HALF_LANES - half), F32)
    zero = jnp.zeros((n_pos, HALF_LANES - half), F32)
    c = jnp.concatenate([cos, one, cos, one], axis=1)
    s = jnp.concatenate([-sin, zero, sin, zero], axis=1)
    return c, s


def rope_lane_order(rot):
    half = rot // 2
    return (list(range(half)) + list(range(rot, rot + HALF_LANES - half))
            + list(range(half, rot)) + list(range(rot + HALF_LANES - half, LANES)))


def _rope_group(x, c, s):
    return x * c + pltpu.roll(x, HALF_LANES, axis=1) * s


CONV_HALO = 16


def _conv_kernel(main_ref, prev_ref, next_ref, w_ref, b_ref, o_ref, buf_ref, *, layout, tq):
    i = pl.program_id(0)
    pos, seqlen = layout.pos_and_len(i * tq)
    first = pos == 0
    last = pos + tq == seqlen
    buf_ref[0:CONV_HALO, :] = jnp.where(first, 0.0, prev_ref[...].astype(F32))
    buf_ref[CONV_HALO:CONV_HALO + tq, :] = main_ref[...].astype(F32)
    buf_ref[CONV_HALO + tq:, :] = jnp.where(last, 0.0, next_ref[...].astype(F32))
    acc = jnp.zeros(o_ref.shape, F32) + b_ref[...]
    for k in range(CONV_K):
        start = CONV_HALO + k - CONV_K // 2
        acc = acc + buf_ref[start:start + tq, :] * w_ref[k:k + 1, :]
    o_ref[...] = (acc * (1.0 / (1.0 + jnp.exp(-acc)))).astype(o_ref.dtype)


def conv_silu(zx, col0, conv_w, conv_b, layout, tq=256, tc=512):
    t = zx.shape[0]
    c = conv_w.shape[1]
    tq = _pick(min(layout.l0, layout.l1), tq)
    cb0 = col0 // tc
    hb = tq // CONV_HALO
    nh = t // CONV_HALO
    return pl.pallas_call(
        functools.partial(_conv_kernel, layout=layout, tq=tq),
        grid=(t // tq, c // tc),
        in_specs=[pl.BlockSpec((tq, tc), lambda i, j: (i, cb0 + j)),
                  pl.BlockSpec((CONV_HALO, tc), lambda i, j: (jnp.maximum(i * hb - 1, 0), cb0 + j)),
                  pl.BlockSpec((CONV_HALO, tc), lambda i, j: (jnp.minimum((i + 1) * hb, nh - 1), cb0 + j)),
                  pl.BlockSpec((CONV_K, tc), lambda i, j: (0, j)),
                  pl.BlockSpec((1, tc), lambda i, j: (0, j))],
        out_specs=pl.BlockSpec((tq, tc), lambda i, j: (i, j)),
        out_shape=jax.ShapeDtypeStruct((t, c), BF16),
        scratch_shapes=[pltpu.VMEM((tq + 2 * CONV_HALO, tc), F32)],
        compiler_params=_cparams(("parallel", "parallel")),
        name="conv_silu",
    )(zx, zx, zx, conv_w.astype(F32), conv_b.reshape(1, c).astype(F32))


def _split3(x):
    hi = x.astype(BF16)
    r1 = x - hi.astype(F32)
    mid = r1.astype(BF16)
    lo = (r1 - mid.astype(F32)).astype(BF16)
    return hi, mid, lo


def _expand_heads(v, e_ref):
    hi, mid, lo = _split3(v)
    e = e_ref[...]
    return (jnp.dot(hi, e, preferred_element_type=F32) + jnp.dot(mid, e, preferred_element_type=F32)
            + jnp.dot(lo, e, preferred_element_type=F32))


def _softplus(x):
    return jnp.maximum(x, 0.0) + jnp.log1p(jnp.exp(-jnp.abs(x)))


def _ssd_kernel(x_ref, b_ref, c_ref, dt_ref, dtt_ref, bias_ref, biast_ref, alog_ref, alogt_ref,
                e_ref, y_ref, s_ref, *, layout, rev):
    q = SSD_CHUNK
    hpg = SSD_HEADS // SSD_GROUPS
    i = pl.program_id(0)
    nchunks = pl.num_programs(0)
    chunk = (nchunks - 1 - i) if rev else i
    pos, seqlen = layout.pos_and_len(chunk * q)
    starts = (pos + q == seqlen) if rev else (pos == 0)

    @pl.when(starts)
    def _():
        s_ref[...] = jnp.zeros_like(s_ref)

    h0 = SSD_HEADS if rev else 0
    dt = _softplus(dt_ref[...] + bias_ref[...])
    da = dt * (-jnp.exp(alog_ref[...]))
    dt_t = _softplus(dtt_ref[...] + biast_ref[...])
    da_t = dt_t * (-jnp.exp(alogt_ref[...]))

    row = lax.broadcasted_iota(jnp.int32, (q, q), 0)
    col = lax.broadcasted_iota(jnp.int32, (q, q), 1)
    mask = (col >= row) if rev else (col <= row)
    tri = jnp.where(mask, 1.0, 0.0).astype(BF16)
    tri_t = jnp.where((row >= col) if rev else (row <= col), 1.0, 0.0).astype(BF16)

    hi, mid, lo = _split3(da)
    cum = (jnp.dot(tri, hi, preferred_element_type=F32) + jnp.dot(tri, mid, preferred_element_type=F32)
           + jnp.dot(tri, lo, preferred_element_type=F32))
    hi, mid, lo = _split3(da_t)
    cum_t = (jnp.dot(hi, tri_t, preferred_element_type=F32) + jnp.dot(mid, tri_t, preferred_element_type=F32)
             + jnp.dot(lo, tri_t, preferred_element_type=F32))

    end = 0 if rev else q - 1
    cum_end = cum[end:end + 1, :]
    ecum = jnp.exp(cum)
    wgt = dt * jnp.exp(cum_end - cum)

    xb = x_ref[...]
    bm = b_ref[...]
    cm = c_ref[...]
    s_in = s_ref[...].astype(BF16)

    for g in range(SSD_GROUPS):
        bg = bm[:, g * SSD_STATE:(g + 1) * SSD_STATE]
        cg = cm[:, g * SSD_STATE:(g + 1) * SSD_STATE]
        cb = lax.dot_general(cg, bg, (((1,), (1,)), ((), ())), preferred_element_type=F32)
        cg32 = cg.astype(F32)
        for pair in range(hpg // 2):
            lane0 = (g * hpg + 2 * pair) * SSD_HEAD_DIM
            rhs = jnp.concatenate([xb[:, lane0:lane0 + LANES], s_in[:, lane0:lane0 + LANES]], axis=0)
            res = []
            for sub in range(2):
                h = h0 + g * hpg + 2 * pair + sub
                diff = cum[:, h:h + 1] - cum_t[h:h + 1, :]
                dec = jnp.exp(jnp.where(mask, diff, NEG_BIG))
                m = (cb * dec * dt_t[h:h + 1, :]).astype(BF16)
                ce = (cg32 * ecum[:, h:h + 1]).astype(BF16)
                lhs = jnp.concatenate([m, ce], axis=1)
                res.append(jnp.dot(lhs, rhs, preferred_element_type=F32))
            lane = lax.broadcasted_iota(jnp.int32, (q, LANES), 1)
            y_ref[:, lane0:lane0 + LANES] = jnp.where(lane < SSD_HEAD_DIM, res[0], res[1])

    xw = (xb.astype(F32) * _expand_heads(wgt, e_ref)).astype(BF16)
    decay8 = _expand_heads(jnp.exp(cum[end - end % 8:end - end % 8 + 8, :]), e_ref)
    decay = decay8[end % 8:end % 8 + 1, :]
    gw = SSD_HEADS // SSD_GROUPS * SSD_HEAD_DIM
    for g in range(SSD_GROUPS):
        bg_t = bm[:, g * SSD_STATE:(g + 1) * SSD_STATE].astype(F32).T.astype(BF16)
        upd = jnp.dot(bg_t, xw[:, g * gw:(g + 1) * gw], preferred_element_type=F32)
        s_ref[:, g * gw:(g + 1) * gw] = s_ref[:, g * gw:(g + 1) * gw] * decay[:, g * gw:(g + 1) * gw] + upd


def ssd_scan(xbc, small, dt_col0, dt_t, dt_bias, a_log, layout, rev):
    t = xbc.shape[0]
    q = SSD_CHUNK
    n = t // q
    ci = (lambda i: n - 1 - i) if rev else (lambda i: i)
    bc_w = SSD_GROUPS * SSD_STATE
    h0 = SSD_HEADS if rev else 0
    expand = jnp.zeros((LANES, D_SSM), BF16).at[h0:h0 + SSD_HEADS].set(
        jnp.repeat(jnp.eye(SSD_HEADS, dtype=BF16), SSD_HEAD_DIM, axis=1))
    full = lambda a: pl.BlockSpec(a.shape, lambda i: (0,) * a.ndim)
    pad = jnp.zeros((LANES - 2 * SSD_HEADS,), F32)
    bias = jnp.concatenate([dt_bias.astype(F32).reshape(-1), pad]).reshape(1, LANES)
    alog = jnp.concatenate([a_log.astype(F32).reshape(-1), pad]).reshape(1, LANES)
    bias_t = dt_bias.astype(F32).reshape(2 * SSD_HEADS, 1)
    alog_t = a_log.astype(F32).reshape(2 * SSD_HEADS, 1)
    args = (xbc, xbc, xbc, small, dt_t, bias, bias_t, alog, alog_t, expand)
    return pl.pallas_call(
        functools.partial(_ssd_kernel, layout=layout, rev=rev),
        grid=(n,),
        in_specs=[pl.BlockSpec((q, D_SSM), lambda i: (ci(i), 0)),
                  pl.BlockSpec((q, bc_w), lambda i: (ci(i), D_SSM // bc_w)),
                  pl.BlockSpec((q, bc_w), lambda i: (ci(i), D_SSM // bc_w + 1)),
                  pl.BlockSpec((q, LANES), lambda i: (ci(i), dt_col0 // LANES)),
                  pl.BlockSpec((2 * SSD_HEADS, q), lambda i: (0, ci(i))),
                  full(bias), full(bias_t), full(alog), full(alog_t), full(expand)],
        out_specs=pl.BlockSpec((q, D_SSM), lambda i: (ci(i), 0)),
        out_shape=jax.ShapeDtypeStruct((t, D_SSM), F32),
        scratch_shapes=[pltpu.VMEM((SSD_STATE, D_SSM), F32)],
        compiler_params=_cparams(("arbitrary",)),
        name="ssd_scan_bwd" if rev else "ssd_scan_fwd",
    )(*args)


def _ssd_combine_kernel(yf_ref, yb_ref, x_ref, z_ref, dskip_ref, g_ref, o_ref):
    x = x_ref[...].astype(F32)
    z = z_ref[...].astype(F32)
    y = (yf_ref[...] + yb_ref[...] + x * dskip_ref[...]) * (z * (1.0 / (1.0 + jnp.exp(-z))))
    gw = D_SSM // SSD_GROUPS
    for g in range(SSD_GROUPS):
        sl = slice(g * gw, (g + 1) * gw)
        o_ref[:, sl] = _rms(y[:, sl], g_ref[:, sl]).astype(o_ref.dtype)


def ssd_combine(y_f, y_b, xbc, zx, d_skip, norm_w, tm=512):
    t = y_f.shape[0]
    tm = _pick(t, tm)
    dvec = jnp.repeat(d_skip.astype(F32), SSD_HEAD_DIM).reshape(1, D_SSM)
    row = pl.BlockSpec((tm, D_SSM), lambda i: (i, 0))
    vec = pl.BlockSpec((1, D_SSM), lambda i: (0, 0))
    return pl.pallas_call(
        _ssd_combine_kernel,
        grid=(t // tm,),
        in_specs=[row, row, row, row, vec, vec],
        out_specs=row,
        out_shape=jax.ShapeDtypeStruct((t, D_SSM), BF16),
        compiler_params=_cparams(("parallel",)),
        name="ssd_combine",
    )(y_f, y_b, xbc, zx, dvec, norm_w.reshape(1, D_SSM).astype(F32))


def _mla_prep_kernel(cq_ref, ckv_ref, kpe_ref, gq_ref, gkv_ref, wq_ref, wk_ref, wvt_ref, c_ref, s_ref,
                     q_ref, k_ref, vt_ref, *, scale):
    c, s = c_ref[...], s_ref[...]
    nw = MLA_HEADS * QK_NOPE
    qn = _rms(cq_ref[...], gq_ref[...]).astype(BF16)
    q = jnp.dot(qn, wq_ref[...], preferred_element_type=F32) * scale
    kvn = _rms(ckv_ref[...], gkv_ref[...]).astype(BF16)
    kn = jnp.dot(kvn, wk_ref[...], preferred_element_type=F32)
    kpe = _rope_group(kpe_ref[...], c, s).astype(BF16)
    for h in range(MLA_HEADS):
        lo = h * 2 * LANES
        q_ref[:, lo:lo + LANES] = q[:, h * LANES:(h + 1) * LANES].astype(BF16)
        qpe = q[:, nw + h * LANES:nw + (h + 1) * LANES]
        q_ref[:, lo + LANES:lo + 2 * LANES] = _rope_group(qpe, c, s).astype(BF16)
        k_ref[:, lo:lo + LANES] = kn[:, h * LANES:(h + 1) * LANES].astype(BF16)
        k_ref[:, lo + LANES:lo + 2 * LANES] = kpe
    vt_ref[...] = lax.dot_general(wvt_ref[...], kvn, (((1,), (1,)), ((), ())),
                                  preferred_element_type=F32).astype(BF16)


def mla_prep(small, gq, gkv, wq, wk, wvt, tabs, layout, tm=256):
    t = small.shape[0]
    tm = _pick(min(layout.l0, layout.l1), tm)
    hq = MLA_HEADS * 2 * LANES
    tab = pl.BlockSpec((tm, LANES), lambda i: (layout.table_block(i, tm), 0))
    full = lambda a: pl.BlockSpec(a.shape, lambda i: (0,) * a.ndim)
    gq2 = gq.reshape(1, Q_LORA).astype(F32)
    gkv2 = gkv.reshape(1, KV_LORA).astype(F32)
    return pl.pallas_call(
        functools.partial(_mla_prep_kernel, scale=LOG2_E / math.sqrt(QK_NOPE + QK_ROPE)),
        grid=(t // tm,),
        in_specs=[pl.BlockSpec((tm, Q_LORA), lambda i: (i, 0)),
                  pl.BlockSpec((tm, KV_LORA), lambda i: (i, 1)),
                  pl.BlockSpec((tm, LANES), lambda i: (i, (Q_LORA + KV_LORA) // LANES)),
                  full(gq2), full(gkv2), full(wq), full(wk), full(wvt), tab, tab],
        out_specs=[pl.BlockSpec((tm, hq), lambda i: (i, 0)),
                   pl.BlockSpec((tm, hq), lambda i: (i, 0)),
                   pl.BlockSpec((MLA_HEADS * V_HEAD, tm), lambda i: (0, i))],
        out_shape=[jax.ShapeDtypeStruct((t, hq), BF16),
                   jax.ShapeDtypeStruct((t, hq), BF16),
                   jax.ShapeDtypeStruct((MLA_HEADS * V_HEAD, t), BF16)],
        compiler_params=_cparams(("parallel",)),
        name="mla_prep",
    )(small, small, small, gq2, gkv2, wq, wk, wvt, *tabs)


def _reduce_rows(x, op):
    rows = x.shape[0]
    while rows > 8 and rows % 8 == 0:
        x = op(x.reshape(8, rows // 8, x.shape[1]), axis=0)
        rows //= 8
    return op(x, axis=0, keepdims=True)


def _flash_kernel(*refs, n_maps, seq_len, bk, lam_init, has_prev):
    if has_prev:
        n_in = 5 if n_maps == 2 else 3
        refs = refs[:n_in] + refs[n_in + 1:]
    if n_maps == 2:
        q_ref, k_ref, vt_ref, lam_ref, subln_ref, o_ref, m_ref, l_ref, acc_ref, sa_ref, sb_ref = refs
    else:
        q_ref, k_ref, vt_ref, o_ref, m_ref, l_ref, acc_ref, sa_ref, sb_ref = refs
    dqk = q_ref.shape[1] // n_maps
    dv = vt_ref.shape[0]
    n_chunks = seq_len // bk
    m_ref[...] = jnp.full_like(m_ref, NEG_BIG)
    l_ref[...] = jnp.zeros_like(l_ref)
    acc_ref[...] = jnp.zeros_like(acc_ref)
    qs = [q_ref[:, t * dqk:(t + 1) * dqk] for t in range(n_maps)]

    def scores(j, t):
        off = pl.multiple_of(j * bk, bk)
        return lax.dot_general(k_ref[pl.ds(off, bk), t * dqk:(t + 1) * dqk], qs[t],
                               (((1,), (1,)), ((), ())), preferred_element_type=F32)

    def accumulate(j, t, s):
        off = pl.multiple_of(j * bk, bk)
        m_old = m_ref[t]
        m_new = jnp.maximum(m_old, _reduce_rows(s, jnp.max))
        alpha = jnp.exp2(m_old - m_new)
        p = jnp.exp2(s - m_new).astype(BF16)
        m_ref[t] = m_new
        vt1 = jnp.concatenate([vt_ref[:, pl.ds(off, bk)], jnp.ones((ONES_ROWS, bk), BF16)], axis=0)
        pv = jnp.dot(vt1, p, preferred_element_type=F32)
        l_ref[t] = alpha * l_ref[t] + pv[dv:dv + 1]
        acc_ref[t] = alpha * acc_ref[t] + pv[:dv]

    def step(j, cur_ref, next_ref, has_next):
        for t in range(n_maps):
            if has_next:
                next_ref[t] = scores(j + 1, t)
            accumulate(j, t, cur_ref[t])

    for t in range(n_maps):
        sa_ref[t] = scores(0, t)
    n_pairs = (n_chunks - 1) // 2

    def body(i, carry):
        step(2 * i, sa_ref, sb_ref, True)
        step(2 * i + 1, sb_ref, sa_ref, True)
        return carry

    lax.fori_loop(0, n_pairs, body, 0)
    bufs = (sa_ref, sb_ref)
    for j in range(2 * n_pairs, n_chunks):
        step(j, bufs[0], bufs[1], j + 1 < n_chunks)
        bufs = (bufs[1], bufs[0])

    if n_maps == 1:
        o_ref[...] = (acc_ref[0] / l_ref[0]).T.astype(o_ref.dtype)
    else:
        lp = lam_ref[...]
        lam = (jnp.exp(jnp.sum(lp[0:1] * lp[1:2], axis=-1, keepdims=True))
               - jnp.exp(jnp.sum(lp[2:3] * lp[3:4], axis=-1, keepdims=True)) + lam_init)
        o = (acc_ref[0] / l_ref[0] - lam * (acc_ref[1] / l_ref[1])).T
        o_ref[...] = (_rms(o, subln_ref[...]) * (1.0 - lam_init)).astype(o_ref.dtype)


def flash_attention(q, k, vt, n_heads, n_maps, q_col0, k_col0, v_row0, row0, n_seq, seq_len,
                    lam=None, subln=None, lam_init=0.0, prev=None, bq=512, bk=512):
    hq = 2 * LANES
    dv = LANES * n_maps
    bq = _pick(seq_len, bq)
    bk = _pick(seq_len, bk)
    nq = seq_len // bq
    assert row0 % seq_len == 0 and q_col0 % hq == 0 and k_col0 % hq == 0 and v_row0 % dv == 0
    qb0, sb0 = row0 // bq, row0 // seq_len
    in_specs = [pl.BlockSpec((bq, hq), lambda b, h, i: (qb0 + b * nq + i, q_col0 // hq + h)),
                pl.BlockSpec((seq_len, hq), lambda b, h, i: (sb0 + b, k_col0 // hq + h)),
                pl.BlockSpec((dv, seq_len), lambda b, h, i: (v_row0 // dv + h, sb0 + b))]
    args = [q, k, vt]
    if n_maps == 2:
        in_specs += [pl.BlockSpec(lam.shape, lambda b, h, i: (0, 0)),
                     pl.BlockSpec(subln.shape, lambda b, h, i: (0, 0))]
        args += [lam, subln]
    aliases = {}
    if prev is not None:
        in_specs.append(pl.BlockSpec(memory_space=pl.ANY))
        aliases = {len(args): 0}
        args.append(prev)
    return pl.pallas_call(
        functools.partial(_flash_kernel, n_maps=n_maps, seq_len=seq_len, bk=bk, lam_init=lam_init,
                          has_prev=prev is not None),
        grid=(n_seq, n_heads, nq),
        in_specs=in_specs,
        out_specs=pl.BlockSpec((bq, dv), lambda b, h, i: (qb0 + b * nq + i, h)),
        out_shape=jax.ShapeDtypeStruct((q.shape[0], n_heads * dv), BF16),
        scratch_shapes=[pltpu.VMEM((n_maps, 1, bq), F32), pltpu.VMEM((n_maps, 1, bq), F32),
                        pltpu.VMEM((n_maps, dv, bq), F32),
                        pltpu.VMEM((n_maps, bk, bq), F32), pltpu.VMEM((n_maps, bk, bq), F32)],
        input_output_aliases=aliases,
        compiler_params=_cparams(("parallel", "parallel", "parallel")),
        name="diff_attention" if n_maps == 2 else "mla_attention",
    )(*args)


def attention_all(layout, **kw):
    out = None
    if layout.n0:
        out = flash_attention(row0=0, n_seq=layout.n0, seq_len=layout.l0, prev=out, **kw)
    if layout.n1:
        out = flash_attention(row0=layout.r0, n_seq=layout.n1, seq_len=layout.l1, prev=out, **kw)
    return out


def _even_in_weight(w):
    o1 = D_SSM
    o2 = o1 + CONV_DIM
    o3 = o2 + 2 * SSD_HEADS
    o4 = o3 + Q_LORA
    o5 = o4 + KV_LORA
    d = w.shape[0]
    zeros = lambda n: jnp.zeros((d, n), w.dtype)
    k_pe = jnp.concatenate([w[:, o5:], zeros(LANES - QK_ROPE)], axis=1)[:, jnp.array(rope_lane_order(QK_ROPE))]
    return jnp.concatenate([w[:, :o2], w[:, o3:o4], w[:, o4:o5], k_pe,
                            w[:, o2:o3], zeros(LANES - 2 * SSD_HEADS)], axis=1).astype(BF16)


def _mla_up_weights(w_q_up, w_kv_up):
    wq = w_q_up.reshape(Q_LORA, MLA_HEADS, QK_NOPE + QK_ROPE)
    q_nope = wq[:, :, :QK_NOPE].reshape(Q_LORA, MLA_HEADS * QK_NOPE)
    q_pe = jnp.pad(wq[:, :, QK_NOPE:], ((0, 0), (0, 0), (0, LANES - QK_ROPE)))
    q_pe = q_pe[:, :, jnp.array(rope_lane_order(QK_ROPE))].reshape(Q_LORA, MLA_HEADS * LANES)
    wkv = w_kv_up.reshape(KV_LORA, MLA_HEADS, QK_NOPE + V_HEAD)
    k_nope = wkv[:, :, :QK_NOPE].reshape(KV_LORA, MLA_HEADS * QK_NOPE)
    v = wkv[:, :, QK_NOPE:].reshape(KV_LORA, MLA_HEADS * V_HEAD)
    return (jnp.concatenate([q_nope, q_pe], axis=1).astype(BF16), k_nope.astype(BF16), v.T.astype(BF16))


EVEN_TN = 640


def _even_mixer(h, g, p, i, layout, tabs):
    t = h.shape[0]
    w_in = _even_in_weight(p['w_in_even'][i])
    n_main, n_small = D_SSM + CONV_DIM, Q_LORA + KV_LORA + 2 * LANES
    jm = n_main // EVEN_TN
    segs = [Seg(0, jm, 'plain', 0), Seg(jm, jm + n_small // EVEN_TN, 'plain', 1)]
    zx, small = norm_proj(h, g, w_in, segs, [((t, n_main), BF16), ((t, n_small), F32)], EVEN_TN, layout)
    dt_col0 = Q_LORA + KV_LORA + LANES

    xbc = conv_silu(zx, D_SSM, p['conv_w'][i], p['conv_b'][i], layout)
    dt_t = small[:, dt_col0:dt_col0 + 2 * SSD_HEADS].T
    y_f = ssd_scan(xbc, small, dt_col0, dt_t, p['dt_bias'][i], p['a_log'][i], layout, rev=False)
    y_b = ssd_scan(xbc, small, dt_col0, dt_t, p['dt_bias'][i], p['a_log'][i], layout, rev=True)
    y = ssd_combine(y_f, y_b, xbc, zx, p['d_skip'][i], p['ssm_norm_w'][i])

    wq, wk, wvt = _mla_up_weights(p['w_q_up'][i], p['w_kv_up'][i])
    q, k, vt = mla_prep(small, p['q_norm_w'][i], p['kv_norm_w'][i], wq, wk, wvt, tabs, layout)
    o = attention_all(layout, q=q, k=k, vt=vt, n_heads=MLA_HEADS, n_maps=1, q_col0=0, k_col0=0, v_row0=0)
    w_out = p['w_out_even'][i].astype(BF16)
    return (y, o), (w_out[:D_SSM], w_out[D_SSM:])


DIFF_TN = 512


def _diff_mixer(h, g, p, j, layer_idx, layout, tabs):
    t = h.shape[0]
    d_diff = DIFF_HEADS * 2 * DIFF_HEAD_DIM
    w_in = p['w_in_odd'][j].astype(BF16)
    order = jnp.array(rope_lane_order(DIFF_ROT))
    w_qk = w_in[:, :2 * d_diff].reshape(-1, 2 * d_diff // LANES, LANES)[:, :, order].reshape(-1, 2 * d_diff)
    jq = d_diff // DIFF_TN
    segs = [Seg(0, jq, 'rope', 0, scale=LOG2_E / math.sqrt(DIFF_HEAD_DIM)),
            Seg(jq, 2 * jq, 'rope', 0),
            Seg(2 * jq, 3 * jq, 'trans', 1)]
    qk, vt = norm_proj(h, g, w_qk, segs, [((t, 2 * d_diff), BF16), ((d_diff, t), BF16)],
                       DIFF_TN, layout, wt=w_in[:, 2 * d_diff:].T, tabs=tabs)
    lam_init = 0.8 - 0.6 * math.exp(-0.3 * layer_idx)
    o = attention_all(layout, q=qk, k=qk, vt=vt, n_heads=DIFF_HEADS, n_maps=2, q_col0=0, k_col0=d_diff,
                      v_row0=0, lam=p['diff_lambda'][j].astype(F32),
                      subln=p['subln_w'][j].reshape(1, -1).astype(F32), lam_init=lam_init)
    return (o,), (p['w_out_odd'][j].astype(BF16),)


def kernel(x_prompt, x_sample, norm_pre_mix, norm_post_mix, norm_pre_mlp, norm_post_mlp,
           w_in_even, conv_w, conv_b, dt_bias, a_log, d_skip, ssm_norm_w, q_norm_w, w_q_up,
           kv_norm_w, w_kv_up, w_out_even, w_in_odd, diff_lambda, subln_w, w_out_odd,
           w_mlp_up, w_mlp_down):
    p = dict(w_in_even=w_in_even, conv_w=conv_w, conv_b=conv_b, dt_bias=dt_bias, a_log=a_log,
             d_skip=d_skip, ssm_norm_w=ssm_norm_w, q_norm_w=q_norm_w, w_q_up=w_q_up,
             kv_norm_w=kv_norm_w, w_kv_up=w_kv_up, w_out_even=w_out_even, w_in_odd=w_in_odd,
             diff_lambda=diff_lambda, subln_w=subln_w, w_out_odd=w_out_odd)
    n0, l0, d = x_prompt.shape
    n1, l1, _ = x_sample.shape
    layout = SeqLayout(n0, l0, n1, l1)
    depth = norm_pre_mix.shape[0]
    n_pos = max(l0, l1)
    tabs_mla = rope_tables(QK_ROPE // 2, n_pos)
    tabs_diff = rope_tables(DIFF_ROT // 2, n_pos)

    h = jnp.concatenate([x_prompt.reshape(n0 * l0, d), x_sample.reshape(n1 * l1, d)], axis=0)
    for i in range(depth):
        if i % 2 == 0:
            acts, ws = _even_mixer(h, norm_pre_mix[i], p, i // 2, layout, tabs_mla)
        else:
            acts, ws = _diff_mixer(h, norm_pre_mix[i], p, i // 2, i, layout, tabs_diff)
        h = outproj_postnorm_residual(acts, ws, norm_post_mix[i], h)
        out_rows = (n0 * l0, n1 * l1) if i == depth - 1 else None
        outs = mlp_block(h, norm_pre_mlp[i], w_mlp_up[i].astype(BF16), w_mlp_down[i].astype(BF16),
                         norm_post_mlp[i], out_rows=out_rows)
        h = outs[0]
    parts = iter(outs)
    y_prompt = next(parts).reshape(n0, l0, d) if n0 * l0 else jnp.zeros((n0, l0, d), F32)
    y_sample = next(parts).reshape(n1, l1, d) if n1 * l1 else jnp.zeros((n1, l1, d), F32)
    return (y_prompt, y_sample)
```

```python
import functools
import math

import jax
import jax.numpy as jnp
from jax import lax
from jax.experimental import pallas as pl
from jax.experimental.pallas import tpu as pltpu

F32 = jnp.float32
BF16 = jnp.bfloat16

EPS = 1e-6
ROPE_THETA = 500000.0
LANES = 128
VMEM_LIMIT_BYTES = 56 * 1024 * 1024

SSD_HEADS = 16
SSD_HEAD_DIM = 64
SSD_GROUPS = 2
SSD_STATE = 128
D_SSM = SSD_HEADS * SSD_HEAD_DIM
CONV_K = 5
CONV_DIM = D_SSM + 2 * SSD_GROUPS * SSD_STATE
MLA_HEADS = 8
QK_NOPE = 128
QK_ROPE = 64
V_HEAD = 128
Q_LORA = 512
KV_LORA = 512
DIFF_HEADS = 8
DIFF_HEAD_DIM = 128
DIFF_ROT = DIFF_HEAD_DIM // 4
SSD_CHUNK = 128
NEG_BIG = -1e30
LOG2_E = math.log2(math.e)
ONES_ROWS = 16
ATTN_UNROLL = 8


def _cparams(sem):
    return pltpu.CompilerParams(dimension_semantics=sem, vmem_limit_bytes=VMEM_LIMIT_BYTES)


def _pick(n, pref):
    t = min(n, pref)
    while n % t:
        t //= 2
    return t


def _rms(x, g):
    return x * lax.rsqrt(jnp.mean(x * x, axis=-1, keepdims=True) + EPS) * g


class Seg:
    def __init__(self, j0, j1, kind, out, scale=1.0):
        self.j0, self.j1, self.kind, self.out, self.scale = j0, j1, kind, out, scale


def _row_sources(parts, tm, grid_rank):
    specs, starts = [], []
    start = 0
    for a in parts:
        nb = a.shape[0] // tm
        if grid_rank == 1:
            imap = lambda i, start=start, nb=nb: (jnp.clip(i - start, 0, nb - 1), 0)
        else:
            imap = lambda i, j, start=start, nb=nb: (jnp.clip(i - start, 0, nb - 1), 0)
        specs.append(pl.BlockSpec((tm, a.shape[1]), imap))
        starts.append(start)
        start += nb
    return specs, tuple(starts)


def _for_row_source(refs, starts, fn, extra_pred=None):
    i = pl.program_id(0)
    ends = list(starts[1:]) + [pl.num_programs(0)]
    for ref, start, end in zip(refs, starts, ends):
        pred = (i >= start) & (i < end)
        if extra_pred is not None:
            pred = pred & extra_pred
        pl.when(pred)(functools.partial(fn, ref))


def _proj_kernel(*refs, segs, n_out, has_wt, has_rope, starts):
    refs = list(refs)
    x_refs = refs[:len(starts)]
    refs = refs[len(starts) - 1:]
    g_ref, w_ref = refs[1:3]
    pos = 3
    wt_ref = None
    if has_wt:
        wt_ref = refs[pos]
        pos += 1
    if has_rope:
        c_ref, s_ref = refs[pos:pos + 2]
        pos += 2
    out_refs = refs[pos:pos + n_out]
    xn_ref = refs[pos + n_out]
    j = pl.program_id(1)

    def normalise(x_ref):
        xn_ref[...] = _rms(x_ref[...], g_ref[...]).astype(BF16)

    _for_row_source(x_refs, starts, normalise, extra_pred=j == 0)

    for seg in segs:
        o_ref = out_refs[seg.out]

        @pl.when((j >= seg.j0) & (j < seg.j1))
        def _(seg=seg, o_ref=o_ref):
            if seg.kind == 'trans':
                o_ref[...] = lax.dot_general(wt_ref[...], xn_ref[...], (((1,), (1,)), ((), ())),
                                             preferred_element_type=F32).astype(o_ref.dtype)
                return
            acc = jnp.dot(xn_ref[...], w_ref[...], preferred_element_type=F32)
            if seg.kind == 'plain':
                o_ref[...] = acc.astype(o_ref.dtype)
            else:
                c, s = c_ref[...] * seg.scale, s_ref[...] * seg.scale
                for gi in range(acc.shape[1] // LANES):
                    sl = slice(gi * LANES, (gi + 1) * LANES)
                    o_ref[:, sl] = _rope_group(acc[:, sl], c, s).astype(o_ref.dtype)


def norm_proj(h, g, w, segs, outs, tn, layout, wt=None, tabs=None, tm=1024):
    t, d = sum(a.shape[0] for a in h), h[0].shape[1]
    tm = _pick(min(layout.l0, layout.l1, tm), tm)
    n_tiles = segs[-1].j1
    has_wt, has_rope = wt is not None, tabs is not None
    w_tiles = w.shape[1] // tn
    x_specs, starts = _row_sources(h, tm, 2)
    in_specs = x_specs + [pl.BlockSpec((1, d), lambda i, j: (0, 0)),
                          pl.BlockSpec((d, tn), lambda i, j: (0, jnp.minimum(j, w_tiles - 1)))]
    args = list(h) + [g.reshape(1, d).astype(F32), w]
    if has_wt:
        in_specs.append(pl.BlockSpec((tn, d), lambda i, j: (jnp.maximum(j - w_tiles, 0), 0)))
        args.append(wt)
    if has_rope:
        tab = pl.BlockSpec((tm, LANES), lambda i, j: (layout.table_block(i, tm), 0))
        in_specs += [tab, tab]
        args += list(tabs)
    out_specs, out_shapes = [], []
    for k, (shape, dtype) in enumerate(outs):
        mine = [s for s in segs if s.out == k]
        j0, j1 = mine[0].j0, mine[-1].j1
        if mine[0].kind == 'trans':
            out_specs.append(pl.BlockSpec((tn, tm), lambda i, j, j0=j0, j1=j1: (jnp.clip(j - j0, 0, j1 - j0 - 1), i)))
        else:
            out_specs.append(pl.BlockSpec((tm, tn), lambda i, j, j0=j0, j1=j1: (i, jnp.clip(j - j0, 0, j1 - j0 - 1))))
        out_shapes.append(jax.ShapeDtypeStruct(shape, dtype))
    return pl.pallas_call(
        functools.partial(_proj_kernel, segs=segs, n_out=len(outs), has_wt=has_wt, has_rope=has_rope,
                          starts=starts),
        grid=(t // tm, n_tiles),
        in_specs=in_specs,
        out_specs=out_specs,
        out_shape=out_shapes,
        scratch_shapes=[pltpu.VMEM((tm, d), BF16)],
        compiler_params=_cparams(("parallel", "arbitrary")),
        name="norm_proj",
    )(*args)


def _outproj_kernel(*refs, n_in, starts):
    a_refs = refs[:n_in]
    w_refs = refs[n_in:2 * n_in]
    g_ref = refs[2 * n_in]
    h_refs = refs[2 * n_in + 1:-1]
    o_ref = refs[-1]
    acc = jnp.dot(a_refs[0][...], w_refs[0][...], preferred_element_type=F32)
    for a_ref, w_ref in zip(a_refs[1:], w_refs[1:]):
        acc = acc + jnp.dot(a_ref[...], w_ref[...], preferred_element_type=F32)
    branch = _rms(acc, g_ref[...])

    def add_residual(h_ref):
        o_ref[...] = h_ref[...] + branch

    _for_row_source(h_refs, starts, add_residual)


def outproj_postnorm_residual(acts, ws, g, h, tm=512):
    t, d = sum(a.shape[0] for a in h), h[0].shape[1]
    tm = _pick(min(a.shape[0] for a in h), tm)
    n_in = len(acts)
    h_specs, starts = _row_sources(h, tm, 1)
    in_specs = ([pl.BlockSpec((tm, a.shape[1]), lambda i: (i, 0)) for a in acts]
                + [pl.BlockSpec(w.shape, lambda i: (0, 0)) for w in ws]
                + [pl.BlockSpec((1, d), lambda i: (0, 0))] + h_specs)
    return pl.pallas_call(
        functools.partial(_outproj_kernel, n_in=n_in, starts=starts),
        grid=(t // tm,),
        in_specs=in_specs,
        out_specs=pl.BlockSpec((tm, d), lambda i: (i, 0)),
        out_shape=jax.ShapeDtypeStruct((t, d), F32),
        compiler_params=_cparams(("parallel",)),
        name="outproj_postnorm_residual",
    )(*acts, *ws, g.reshape(1, d).astype(F32), *h)


def _mlp_kernel(h_ref, gpre_ref, wup_ref, wdn_ref, gpost_ref, *rest, block_starts):
    out_refs = rest[:len(block_starts)]
    xn_ref, acc_ref = rest[len(block_starts):]
    i = pl.program_id(0)
    j = pl.program_id(1)

    @pl.when(j == 0)
    def _():
        xn_ref[...] = _rms(h_ref[...], gpre_ref[...]).astype(BF16)
        acc_ref[...] = jnp.zeros_like(acc_ref)

    a = jnp.dot(xn_ref[...], wup_ref[...], preferred_element_type=F32)
    a = jnp.square(jnp.maximum(a, 0.0)).astype(BF16)
    acc_ref[...] += jnp.dot(a, wdn_ref[...], preferred_element_type=F32)

    ends = list(block_starts[1:]) + [pl.num_programs(0)]
    for o_ref, start, end in zip(out_refs, block_starts, ends):
        @pl.when((j == pl.num_programs(1) - 1) & (i >= start) & (i < end))
        def _(o_ref=o_ref):
            o_ref[...] = h_ref[...] + _rms(acc_ref[...], gpost_ref[...])


def mlp_block(h, g_pre, w_up, w_down, g_post, out_rows=None, tm=512, tf=1024):
    t, d = h.shape
    f = w_up.shape[1]
    out_rows = (t,) if out_rows is None else tuple(r for r in out_rows if r)
    tm = _pick(min(out_rows), tm)
    tf = _pick(f, tf)
    block_starts, out_specs = [], []
    start = 0
    for r in out_rows:
        nb = r // tm
        out_specs.append(pl.BlockSpec((tm, d), lambda i, j, start=start, nb=nb: (jnp.clip(i - start, 0, nb - 1), 0)))
        block_starts.append(start)
        start += nb
    outs = pl.pallas_call(
        functools.partial(_mlp_kernel, block_starts=tuple(block_starts)),
        grid=(t // tm, f // tf),
        in_specs=[pl.BlockSpec((tm, d), lambda i, j: (i, 0)),
                  pl.BlockSpec((1, d), lambda i, j: (0, 0)),
                  pl.BlockSpec((d, tf), lambda i, j: (0, j)),
                  pl.BlockSpec((tf, d), lambda i, j: (j, 0)),
                  pl.BlockSpec((1, d), lambda i, j: (0, 0))],
        out_specs=out_specs,
        out_shape=[jax.ShapeDtypeStruct((r, d), F32) for r in out_rows],
        scratch_shapes=[pltpu.VMEM((tm, d), BF16), pltpu.VMEM((tm, d), F32)],
        compiler_params=_cparams(("arbitrary", "arbitrary")),
        name="mlp_block",
    )(h, g_pre.reshape(1, d).astype(F32), w_up, w_down, g_post.reshape(1, d).astype(F32))
    return outs


class SeqLayout:
    def __init__(self, n0, l0, n1, l1):
        self.n0, self.l0, self.n1, self.l1 = n0, l0, n1, l1
        self.r0 = n0 * l0
        self.t = self.r0 + n1 * l1
        assert self.r0 % l1 == 0, "sample sequences must start on a multiple of their length"

    def pos_and_len(self, row):
        in0 = row < self.r0
        pos = jnp.where(in0, lax.rem(row, self.l0), lax.rem(row - self.r0, self.l1))
        return pos, jnp.where(in0, self.l0, self.l1)

    def table_block(self, i, tm):
        b0 = self.r0 // tm
        return jnp.where(i < b0, lax.rem(i, self.l0 // tm), lax.rem(i - b0, self.l1 // tm))


HALF_LANES = LANES // 2


def rope_tables(half, n_pos):
    inv_freq = ROPE_THETA ** (-jnp.arange(half, dtype=F32) / half)
    ang = jnp.arange(n_pos, dtype=F32)[:, None] * inv_freq[None, :]
    cos, sin = jnp.cos(ang), jnp.sin(ang)
    one = jnp.ones((n_pos, HALF_LANES - half), F32)
    zero = jnp.zeros((n_pos, HALF_LANES - half), F32)
    c = jnp.concatenate([cos, one, cos, one], axis=1)
    s = jnp.concatenate([-sin, zero, sin, zero], axis=1)
    return c, s


def rope_lane_order(rot):
    half = rot // 2
    return (list(range(half)) + list(range(rot, rot + HALF_LANES - half))
            + list(range(half, rot)) + list(range(rot + HALF_LANES - half, LANES)))


def _rope_group(x, c, s):
    return x * c + pltpu.roll(x, HALF_LANES, axis=1) * s


CONV_HALO = 16


def _conv_kernel(main_ref, prev_ref, next_ref, w_ref, b_ref, o_ref, buf_ref, *, layout, tq):
    i = pl.program_id(0)
    pos, seqlen = layout.pos_and_len(i * tq)
    first = pos == 0
    last = pos + tq == seqlen
    buf_ref[0:CONV_HALO, :] = jnp.where(first, 0.0, prev_ref[...].astype(F32))
    buf_ref[CONV_HALO:CONV_HALO + tq, :] = main_ref[...].astype(F32)
    buf_ref[CONV_HALO + tq:, :] = jnp.where(last, 0.0, next_ref[...].astype(F32))
    acc = jnp.zeros(o_ref.shape, F32) + b_ref[...]
    for k in range(CONV_K):
        start = CONV_HALO + k - CONV_K // 2
        acc = acc + buf_ref[start:start + tq, :] * w_ref[k:k + 1, :]
    o_ref[...] = (acc * (1.0 / (1.0 + jnp.exp(-acc)))).astype(o_ref.dtype)


def conv_silu(zx, col0, conv_w, conv_b, layout, tq=512, tc=512):
    t = zx.shape[0]
    c = conv_w.shape[1]
    tq = _pick(min(layout.l0, layout.l1), tq)
    cb0 = col0 // tc
    hb = tq // CONV_HALO
    nh = t // CONV_HALO
    return pl.pallas_call(
        functools.partial(_conv_kernel, layout=layout, tq=tq),
        grid=(t // tq, c // tc),
        in_specs=[pl.BlockSpec((tq, tc), lambda i, j: (i, cb0 + j)),
                  pl.BlockSpec((CONV_HALO, tc), lambda i, j: (jnp.maximum(i * hb - 1, 0), cb0 + j)),
                  pl.BlockSpec((CONV_HALO, tc), lambda i, j: (jnp.minimum((i + 1) * hb, nh - 1), cb0 + j)),
                  pl.BlockSpec((CONV_K, tc), lambda i, j: (0, j)),
                  pl.BlockSpec((1, tc), lambda i, j: (0, j))],
        out_specs=pl.BlockSpec((tq, tc), lambda i, j: (i, j)),
        out_shape=jax.ShapeDtypeStruct((t, c), BF16),
        scratch_shapes=[pltpu.VMEM((tq + 2 * CONV_HALO, tc), F32)],
        compiler_params=_cparams(("parallel", "parallel")),
        name="conv_silu",
    )(zx, zx, zx, conv_w.astype(F32), conv_b.reshape(1, c).astype(F32))


def _split3(x):
    hi = x.astype(BF16)
    r1 = x - hi.astype(F32)
    mid = r1.astype(BF16)
    lo = (r1 - mid.astype(F32)).astype(BF16)
    return hi, mid, lo


def _expand_heads(v, e_ref):
    hi, mid, lo = _split3(v)
    e = e_ref[...]
    return (jnp.dot(hi, e, preferred_element_type=F32) + jnp.dot(mid, e, preferred_element_type=F32)
            + jnp.dot(lo, e, preferred_element_type=F32))


def _softplus(x):
    return jnp.maximum(x, 0.0) + jnp.log1p(jnp.exp(-jnp.abs(x)))


def _ssd_kernel(x_ref, b_ref, c_ref, dt_ref, dtt_ref, bias_ref, biast_ref, alog_ref, alogt_ref,
                e_ref, y_ref, s_ref, *, layout, rev):
    q = SSD_CHUNK
    hpg = SSD_HEADS // SSD_GROUPS
    i = pl.program_id(0)
    nchunks = pl.num_programs(0)
    chunk = (nchunks - 1 - i) if rev else i
    pos, seqlen = layout.pos_and_len(chunk * q)
    starts = (pos + q == seqlen) if rev else (pos == 0)

    @pl.when(starts)
    def _():
        s_ref[...] = jnp.zeros_like(s_ref)

    h0 = SSD_HEADS if rev else 0
    dt = _softplus(dt_ref[...] + bias_ref[...])
    da = dt * (-jnp.exp(alog_ref[...]))
    dt_t = _softplus(dtt_ref[...] + biast_ref[...])
    da_t = dt_t * (-jnp.exp(alogt_ref[...]))

    row = lax.broadcasted_iota(jnp.int32, (q, q), 0)
    col = lax.broadcasted_iota(jnp.int32, (q, q), 1)
    mask = (col >= row) if rev else (col <= row)
    tri = jnp.where(mask, 1.0, 0.0).astype(BF16)
    tri_t = jnp.where((row >= col) if rev else (row <= col), 1.0, 0.0).astype(BF16)

    hi, mid, lo = _split3(da)
    cum = (jnp.dot(tri, hi, preferred_element_type=F32) + jnp.dot(tri, mid, preferred_element_type=F32)
           + jnp.dot(tri, lo, preferred_element_type=F32))
    hi, mid, lo = _split3(da_t)
    cum_t = (jnp.dot(hi, tri_t, preferred_element_type=F32) + jnp.dot(mid, tri_t, preferred_element_type=F32)
             + jnp.dot(lo, tri_t, preferred_element_type=F32))

    end = 0 if rev else q - 1
    cum_end = cum[end:end + 1, :]
    ecum = jnp.exp(cum)
    wgt = dt * jnp.exp(cum_end - cum)

    xb = x_ref[...]
    bm = b_ref[...]
    cm = c_ref[...]
    s_in = s_ref[...].astype(BF16)

    for g in range(SSD_GROUPS):
        bg = bm[:, g * SSD_STATE:(g + 1) * SSD_STATE]
        cg = cm[:, g * SSD_STATE:(g + 1) * SSD_STATE]
        cb = lax.dot_general(cg, bg, (((1,), (1,)), ((), ())), preferred_element_type=F32)
        cg32 = cg.astype(F32)
        for pair in range(hpg // 2):
            lane0 = (g * hpg + 2 * pair) * SSD_HEAD_DIM
            rhs = jnp.concatenate([xb[:, lane0:lane0 + LANES], s_in[:, lane0:lane0 + LANES]], axis=0)
            res = []
            for sub in range(2):
                h = h0 + g * hpg + 2 * pair + sub
                diff = cum[:, h:h + 1] - cum_t[h:h + 1, :]
                dec = jnp.exp(jnp.where(mask, diff, NEG_BIG))
                m = (cb * dec * dt_t[h:h + 1, :]).astype(BF16)
                ce = (cg32 * ecum[:, h:h + 1]).astype(BF16)
                lhs = jnp.concatenate([m, ce], axis=1)
                res.append(jnp.dot(lhs, rhs, preferred_element_type=F32))
            lane = lax.broadcasted_iota(jnp.int32, (q, LANES), 1)
            y_ref[:, lane0:lane0 + LANES] = jnp.where(lane < SSD_HEAD_DIM, res[0], res[1])

    xw = (xb.astype(F32) * _expand_heads(wgt, e_ref)).astype(BF16)
    decay8 = _expand_heads(jnp.exp(cum[end - end % 8:end - end % 8 + 8, :]), e_ref)
    decay = decay8[end % 8:end % 8 + 1, :]
    gw = SSD_HEADS // SSD_GROUPS * SSD_HEAD_DIM
    for g in range(SSD_GROUPS):
        bg_t = bm[:, g * SSD_STATE:(g + 1) * SSD_STATE].astype(F32).T.astype(BF16)
        upd = jnp.dot(bg_t, xw[:, g * gw:(g + 1) * gw], preferred_element_type=F32)
        s_ref[:, g * gw:(g + 1) * gw] = s_ref[:, g * gw:(g + 1) * gw] * decay[:, g * gw:(g + 1) * gw] + upd


def ssd_scan(xbc, small, dt_col0, dt_t, dt_bias, a_log, layout, rev):
    t = xbc.shape[0]
    q = SSD_CHUNK
    n = t // q
    ci = (lambda i: n - 1 - i) if rev else (lambda i: i)
    bc_w = SSD_GROUPS * SSD_STATE
    h0 = SSD_HEADS if rev else 0
    expand = jnp.zeros((LANES, D_SSM), BF16).at[h0:h0 + SSD_HEADS].set(
        jnp.repeat(jnp.eye(SSD_HEADS, dtype=BF16), SSD_HEAD_DIM, axis=1))
    full = lambda a: pl.BlockSpec(a.shape, lambda i: (0,) * a.ndim)
    pad = jnp.zeros((LANES - 2 * SSD_HEADS,), F32)
    bias = jnp.concatenate([dt_bias.astype(F32).reshape(-1), pad]).reshape(1, LANES)
    alog = jnp.concatenate([a_log.astype(F32).reshape(-1), pad]).reshape(1, LANES)
    bias_t = dt_bias.astype(F32).reshape(2 * SSD_HEADS, 1)
    alog_t = a_log.astype(F32).reshape(2 * SSD_HEADS, 1)
    args = (xbc, xbc, xbc, small, dt_t, bias, bias_t, alog, alog_t, expand)
    return pl.pallas_call(
        functools.partial(_ssd_kernel, layout=layout, rev=rev),
        grid=(n,),
        in_specs=[pl.BlockSpec((q, D_SSM), lambda i: (ci(i), 0)),
                  pl.BlockSpec((q, bc_w), lambda i: (ci(i), D_SSM // bc_w)),
                  pl.BlockSpec((q, bc_w), lambda i: (ci(i), D_SSM // bc_w + 1)),
                  pl.BlockSpec((q, LANES), lambda i: (ci(i), dt_col0 // LANES)),
                  pl.BlockSpec((2 * SSD_HEADS, q), lambda i: (0, ci(i))),
                  full(bias), full(bias_t), full(alog), full(alog_t), full(expand)],
        out_specs=pl.BlockSpec((q, D_SSM), lambda i: (ci(i), 0)),
        out_shape=jax.ShapeDtypeStruct((t, D_SSM), F32),
        scratch_shapes=[pltpu.VMEM((SSD_STATE, D_SSM), F32)],
        compiler_params=_cparams(("arbitrary",)),
        name="ssd_scan_bwd" if rev else "ssd_scan_fwd",
    )(*args)


def _ssd_combine_kernel(yf_ref, yb_ref, x_ref, z_ref, dskip_ref, g_ref, o_ref):
    x = x_ref[...].astype(F32)
    z = z_ref[...].astype(F32)
    y = (yf_ref[...] + yb_ref[...] + x * dskip_ref[...]) * (z * (1.0 / (1.0 + jnp.exp(-z))))
    gw = D_SSM // SSD_GROUPS
    for g in range(SSD_GROUPS):
        sl = slice(g * gw, (g + 1) * gw)
        o_ref[:, sl] = _rms(y[:, sl], g_ref[:, sl]).astype(o_ref.dtype)


def ssd_combine(y_f, y_b, xbc, zx, d_skip, norm_w, tm=512):
    t = y_f.shape[0]
    tm = _pick(t, tm)
    dvec = jnp.repeat(d_skip.astype(F32), SSD_HEAD_DIM).reshape(1, D_SSM)
    row = pl.BlockSpec((tm, D_SSM), lambda i: (i, 0))
    vec = pl.BlockSpec((1, D_SSM), lambda i: (0, 0))
    return pl.pallas_call(
        _ssd_combine_kernel,
        grid=(t // tm,),
        in_specs=[row, row, row, row, vec, vec],
        out_specs=row,
        out_shape=jax.ShapeDtypeStruct((t, D_SSM), BF16),
        compiler_params=_cparams(("parallel",)),
        name="ssd_combine",
    )(y_f, y_b, xbc, zx, dvec, norm_w.reshape(1, D_SSM).astype(F32))


def _mla_prep_kernel(cq_ref, ckv_ref, kpe_ref, gq_ref, gkv_ref, wq_ref, wk_ref, wvt_ref, c_ref, s_ref,
                     q_ref, k_ref, vt_ref, *, scale):
    c, s = c_ref[...], s_ref[...]
    nw = MLA_HEADS * QK_NOPE
    qn = _rms(cq_ref[...], gq_ref[...]).astype(BF16)
    q = jnp.dot(qn, wq_ref[...], preferred_element_type=F32) * scale
    kvn = _rms(ckv_ref[...], gkv_ref[...]).astype(BF16)
    kn = jnp.dot(kvn, wk_ref[...], preferred_element_type=F32)
    kpe = _rope_group(kpe_ref[...], c, s).astype(BF16)
    for h in range(MLA_HEADS):
        lo = h * 2 * LANES
        q_ref[:, lo:lo + LANES] = q[:, h * LANES:(h + 1) * LANES].astype(BF16)
        qpe = q[:, nw + h * LANES:nw + (h + 1) * LANES]
        q_ref[:, lo + LANES:lo + 2 * LANES] = _rope_group(qpe, c, s).astype(BF16)
        k_ref[:, lo:lo + LANES] = kn[:, h * LANES:(h + 1) * LANES].astype(BF16)
        k_ref[:, lo + LANES:lo + 2 * LANES] = kpe
    vt_ref[...] = lax.dot_general(wvt_ref[...], kvn, (((1,), (1,)), ((), ())),
                                  preferred_element_type=F32).astype(BF16)


def mla_prep(small, gq, gkv, wq, wk, wvt, tabs, layout, tm=256):
    t = small.shape[0]
    tm = _pick(min(layout.l0, layout.l1), tm)
    hq = MLA_HEADS * 2 * LANES
    tab = pl.BlockSpec((tm, LANES), lambda i: (layout.table_block(i, tm), 0))
    full = lambda a: pl.BlockSpec(a.shape, lambda i: (0,) * a.ndim)
    gq2 = gq.reshape(1, Q_LORA).astype(F32)
    gkv2 = gkv.reshape(1, KV_LORA).astype(F32)
    return pl.pallas_call(
        functools.partial(_mla_prep_kernel, scale=LOG2_E / math.sqrt(QK_NOPE + QK_ROPE)),
        grid=(t // tm,),
        in_specs=[pl.BlockSpec((tm, Q_LORA), lambda i: (i, 0)),
                  pl.BlockSpec((tm, KV_LORA), lambda i: (i, 1)),
                  pl.BlockSpec((tm, LANES), lambda i: (i, (Q_LORA + KV_LORA) // LANES)),
                  full(gq2), full(gkv2), full(wq), full(wk), full(wvt), tab, tab],
        out_specs=[pl.BlockSpec((tm, hq), lambda i: (i, 0)),
                   pl.BlockSpec((tm, hq), lambda i: (i, 0)),
                   pl.BlockSpec((MLA_HEADS * V_HEAD, tm), lambda i: (0, i))],
        out_shape=[jax.ShapeDtypeStruct((t, hq), BF16),
                   jax.ShapeDtypeStruct((t, hq), BF16),
                   jax.ShapeDtypeStruct((MLA_HEADS * V_HEAD, t), BF16)],
        compiler_params=_cparams(("parallel",)),
        name="mla_prep",
    )(small, small, small, gq2, gkv2, wq, wk, wvt, *tabs)


def _reduce_rows(x, op):
    rows = x.shape[0]
    while rows > 8 and rows % 8 == 0:
        x = op(x.reshape(8, rows // 8, x.shape[1]), axis=0)
        rows //= 8
    return op(x, axis=0, keepdims=True)


def _flash_kernel(*refs, n_maps, seq_len, bk, lam_init, has_prev):
    if has_prev:
        n_in = 6 if n_maps == 2 else 4
        refs = refs[:n_in] + refs[n_in + 1:]
    if n_maps == 2:
        q_ref, qn_ref, k_ref, vt_ref, lam_ref, subln_ref, o_ref, m_ref, l_ref, acc_ref, sa_ref, sb_ref = refs
    else:
        q_ref, qn_ref, k_ref, vt_ref, o_ref, m_ref, l_ref, acc_ref, sa_ref, sb_ref = refs
    n_streams = m_ref.shape[0]
    dqk = q_ref.shape[1] // n_maps
    rows = q_ref.shape[0] * n_maps // n_streams
    col = lambda t: slice(t * dqk, (t + 1) * dqk) if n_maps == 2 else slice(None)
    row = lambda t: slice(None) if n_maps == 2 else slice(t * rows, (t + 1) * rows)
    dv = vt_ref.shape[0]
    n_chunks = seq_len // bk
    chained = n_chunks % 2 == 0
    m_ref[...] = jnp.full_like(m_ref, NEG_BIG)
    l_ref[...] = jnp.zeros_like(l_ref)
    acc_ref[...] = jnp.zeros_like(acc_ref)

    def scores(j, t, query_ref=q_ref):
        off = pl.multiple_of(j * bk, bk)
        return lax.dot_general(k_ref[pl.ds(off, bk), col(t)], query_ref[row(t), col(t)],
                               (((1,), (1,)), ((), ())), preferred_element_type=F32)

    def accumulate(j, t, s):
        off = pl.multiple_of(j * bk, bk)
        m_old = m_ref[t]
        m_new = jnp.maximum(m_old, _reduce_rows(s, jnp.max))
        alpha = jnp.exp2(m_old - m_new)
        p = jnp.exp2(s - m_new).astype(BF16)
        m_ref[t] = m_new
        vt1 = jnp.concatenate([vt_ref[:, pl.ds(off, bk)], jnp.ones((ONES_ROWS, bk), BF16)], axis=0)
        pv = jnp.dot(vt1, p, preferred_element_type=F32)
        l_ref[t] = alpha * l_ref[t] + pv[dv:dv + 1]
        acc_ref[t] = alpha * acc_ref[t] + pv[:dv]

    def step(j, cur_ref, next_ref, has_next):
        for t in range(n_streams):
            if has_next:
                next_ref[t] = scores(j + 1, t)
            elif chained:
                next_ref[t] = scores(0, t, qn_ref)
            accumulate(j, t, cur_ref[t])

    def first_scores():
        for t in range(n_streams):
            sa_ref[t] = scores(0, t)

    if chained:
        pl.when(pl.program_id(2) == 0)(first_scores)
    else:
        first_scores()
    bufs = (sa_ref, sb_ref)
    n_loop = (n_chunks - 1) // ATTN_UNROLL

    def body(i, carry):
        for u in range(ATTN_UNROLL):
            step(ATTN_UNROLL * i + u, bufs[u % 2], bufs[1 - u % 2], True)
        return carry

    lax.fori_loop(0, n_loop, body, 0)
    for j in range(ATTN_UNROLL * n_loop, n_chunks):
        step(j, bufs[j % 2], bufs[1 - j % 2], j + 1 < n_chunks)

    if n_maps == 1:
        for t in range(n_streams):
            o_ref[row(t), :] = (acc_ref[t] / l_ref[t]).T.astype(o_ref.dtype)
    else:
        lp = lam_ref[...]
        lam = (jnp.exp(jnp.sum(lp[0:1] * lp[1:2], axis=-1, keepdims=True))
               - jnp.exp(jnp.sum(lp[2:3] * lp[3:4], axis=-1, keepdims=True)) + lam_init)
        o = (acc_ref[0] / l_ref[0] - lam * (acc_ref[1] / l_ref[1])).T
        o_ref[...] = (_rms(o, subln_ref[...]) * (1.0 - lam_init)).astype(o_ref.dtype)


def flash_attention(q, k, vt, n_heads, n_maps, q_col0, k_col0, v_row0, row0, n_seq, seq_len,
                    lam=None, subln=None, lam_init=0.0, prev=None, bq=512, bk=512):
    hq = 2 * LANES
    dv = LANES * n_maps
    bq = _pick(seq_len, bq)
    bk = _pick(seq_len, bk)
    nq = seq_len // bq
    ns = n_maps
    bqs = bq * n_maps // ns
    assert row0 % seq_len == 0 and q_col0 % hq == 0 and k_col0 % hq == 0 and v_row0 % dv == 0
    qb0, sb0 = row0 // bq, row0 // seq_len
    in_specs = [pl.BlockSpec((bq, hq), lambda b, h, i: (qb0 + b * nq + i, q_col0 // hq + h)),
                pl.BlockSpec((bq, hq), lambda b, h, i: (qb0 + b * nq + jnp.minimum(i + 1, nq - 1), q_col0 // hq + h)),
                pl.BlockSpec((seq_len, hq), lambda b, h, i: (sb0 + b, k_col0 // hq + h)),
                pl.BlockSpec((dv, seq_len), lambda b, h, i: (v_row0 // dv + h, sb0 + b))]
    args = [q, q, k, vt]
    if n_maps == 2:
        in_specs += [pl.BlockSpec(lam.shape, lambda b, h, i: (0, 0)),
                     pl.BlockSpec(subln.shape, lambda b, h, i: (0, 0))]
        args += [lam, subln]
    aliases = {}
    if prev is not None:
        in_specs.append(pl.BlockSpec(memory_space=pl.ANY))
        aliases = {len(args): 0}
        args.append(prev)
    return pl.pallas_call(
        functools.partial(_flash_kernel, n_maps=n_maps, seq_len=seq_len, bk=bk, lam_init=lam_init,
                          has_prev=prev is not None),
        grid=(n_seq, n_heads, nq),
        in_specs=in_specs,
        out_specs=pl.BlockSpec((bq, dv), lambda b, h, i: (qb0 + b * nq + i, h)),
        out_shape=jax.ShapeDtypeStruct((q.shape[0], n_heads * dv), BF16),
        scratch_shapes=[pltpu.VMEM((ns, 1, bqs), F32), pltpu.VMEM((ns, 1, bqs), F32),
                        pltpu.VMEM((ns, dv, bqs), F32),
                        pltpu.VMEM((ns, bk, bqs), F32), pltpu.VMEM((ns, bk, bqs), F32)],
        input_output_aliases=aliases,
        compiler_params=_cparams(("parallel", "parallel", "arbitrary")),
        name="diff_attention" if n_maps == 2 else "mla_attention",
    )(*args)


def attention_all(layout, **kw):
    out = None
    if layout.n0:
        out = flash_attention(row0=0, n_seq=layout.n0, seq_len=layout.l0, prev=out, **kw)
    if layout.n1:
        out = flash_attention(row0=layout.r0, n_seq=layout.n1, seq_len=layout.l1, prev=out, **kw)
    return out


def _even_in_weight(w):
    o1 = D_SSM
    o2 = o1 + CONV_DIM
    o3 = o2 + 2 * SSD_HEADS
    o4 = o3 + Q_LORA
    o5 = o4 + KV_LORA
    d = w.shape[0]
    zeros = lambda n: jnp.zeros((d, n), w.dtype)
    k_pe = jnp.concatenate([w[:, o5:], zeros(LANES - QK_ROPE)], axis=1)[:, jnp.array(rope_lane_order(QK_ROPE))]
    return jnp.concatenate([w[:, :o2], w[:, o3:o4], w[:, o4:o5], k_pe,
                            w[:, o2:o3], zeros(LANES - 2 * SSD_HEADS)], axis=1).astype(BF16)


def _mla_up_weights(w_q_up, w_kv_up):
    wq = w_q_up.reshape(Q_LORA, MLA_HEADS, QK_NOPE + QK_ROPE)
    q_nope = wq[:, :, :QK_NOPE].reshape(Q_LORA, MLA_HEADS * QK_NOPE)
    q_pe = jnp.pad(wq[:, :, QK_NOPE:], ((0, 0), (0, 0), (0, LANES - QK_ROPE)))
    q_pe = q_pe[:, :, jnp.array(rope_lane_order(QK_ROPE))].reshape(Q_LORA, MLA_HEADS * LANES)
    wkv = w_kv_up.reshape(KV_LORA, MLA_HEADS, QK_NOPE + V_HEAD)
    k_nope = wkv[:, :, :QK_NOPE].reshape(KV_LORA, MLA_HEADS * QK_NOPE)
    v = wkv[:, :, QK_NOPE:].reshape(KV_LORA, MLA_HEADS * V_HEAD)
    return (jnp.concatenate([q_nope, q_pe], axis=1).astype(BF16), k_nope.astype(BF16), v.T.astype(BF16))


EVEN_TN = 640


def _even_mixer(h, g, p, i, layout, tabs):
    t = layout.t
    w_in = _even_in_weight(p['w_in_even'][i])
    n_main, n_small = D_SSM + CONV_DIM, Q_LORA + KV_LORA + 2 * LANES
    jm = n_main // EVEN_TN
    segs = [Seg(0, jm, 'plain', 0), Seg(jm, jm + n_small // EVEN_TN, 'plain', 1)]
    zx, small = norm_proj(h, g, w_in, segs, [((t, n_main), BF16), ((t, n_small), F32)], EVEN_TN, layout)
    dt_col0 = Q_LORA + KV_LORA + LANES

    xbc = conv_silu(zx, D_SSM, p['conv_w'][i], p['conv_b'][i], layout)
    dt_t = small[:, dt_col0:dt_col0 + 2 * SSD_HEADS].T
    y_f = ssd_scan(xbc, small, dt_col0, dt_t, p['dt_bias'][i], p['a_log'][i], layout, rev=False)
    y_b = ssd_scan(xbc, small, dt_col0, dt_t, p['dt_bias'][i], p['a_log'][i], layout, rev=True)
    y = ssd_combine(y_f, y_b, xbc, zx, p['d_skip'][i], p['ssm_norm_w'][i])

    wq, wk, wvt = _mla_up_weights(p['w_q_up'][i], p['w_kv_up'][i])
    q, k, vt = mla_prep(small, p['q_norm_w'][i], p['kv_norm_w'][i], wq, wk, wvt, tabs, layout)
    o = attention_all(layout, q=q, k=k, vt=vt, n_heads=MLA_HEADS, n_maps=1, q_col0=0, k_col0=0, v_row0=0)
    w_out = p['w_out_even'][i].astype(BF16)
    return (y, o), (w_out[:D_SSM], w_out[D_SSM:])


DIFF_TN = 512


def _diff_mixer(h, g, p, j, layer_idx, layout, tabs):
    t = layout.t
    d_diff = DIFF_HEADS * 2 * DIFF_HEAD_DIM
    w_in = p['w_in_odd'][j].astype(BF16)
    order = jnp.array(rope_lane_order(DIFF_ROT))
    w_qk = w_in[:, :2 * d_diff].reshape(-1, 2 * d_diff // LANES, LANES)[:, :, order].reshape(-1, 2 * d_diff)
    jq = d_diff // DIFF_TN
    segs = [Seg(0, jq, 'rope', 0, scale=LOG2_E / math.sqrt(DIFF_HEAD_DIM)),
            Seg(jq, 2 * jq, 'rope', 0),
            Seg(2 * jq, 3 * jq, 'trans', 1)]
    qk, vt = norm_proj(h, g, w_qk, segs, [((t, 2 * d_diff), BF16), ((d_diff, t), BF16)],
                       DIFF_TN, layout, wt=w_in[:, 2 * d_diff:].T, tabs=tabs)
    lam_init = 0.8 - 0.6 * math.exp(-0.3 * layer_idx)
    o = attention_all(layout, q=qk, k=qk, vt=vt, n_heads=DIFF_HEADS, n_maps=2, q_col0=0, k_col0=d_diff,
                      v_row0=0, lam=p['diff_lambda'][j].astype(F32),
                      subln=p['subln_w'][j].reshape(1, -1).astype(F32), lam_init=lam_init)
    return (o,), (p['w_out_odd'][j].astype(BF16),)


def kernel(x_prompt, x_sample, norm_pre_mix, norm_post_mix, norm_pre_mlp, norm_post_mlp,
           w_in_even, conv_w, conv_b, dt_bias, a_log, d_skip, ssm_norm_w, q_norm_w, w_q_up,
           kv_norm_w, w_kv_up, w_out_even, w_in_odd, diff_lambda, subln_w, w_out_odd,
           w_mlp_up, w_mlp_down):
    p = dict(w_in_even=w_in_even, conv_w=conv_w, conv_b=conv_b, dt_bias=dt_bias, a_log=a_log,
             d_skip=d_skip, ssm_norm_w=ssm_norm_w, q_norm_w=q_norm_w, w_q_up=w_q_up,
             kv_norm_w=kv_norm_w, w_kv_up=w_kv_up, w_out_even=w_out_even, w_in_odd=w_in_odd,
             diff_lambda=diff_lambda, subln_w=subln_w, w_out_odd=w_out_odd)
    n0, l0, d = x_prompt.shape
    n1, l1, _ = x_sample.shape
    layout = SeqLayout(n0, l0, n1, l1)
    depth = norm_pre_mix.shape[0]
    n_pos = max(l0, l1)
    tabs_mla = rope_tables(QK_ROPE // 2, n_pos)
    tabs_diff = rope_tables(DIFF_ROT // 2, n_pos)

    h = tuple(a for a in (x_prompt.reshape(n0 * l0, d), x_sample.reshape(n1 * l1, d)) if a.shape[0])
    for i in range(depth):
        if i % 2 == 0:
            acts, ws = _even_mixer(h, norm_pre_mix[i], p, i // 2, layout, tabs_mla)
        else:
            acts, ws = _diff_mixer(h, norm_pre_mix[i], p, i // 2, i, layout, tabs_diff)
        h = outproj_postnorm_residual(acts, ws, norm_post_mix[i], h)
        out_rows = (n0 * l0, n1 * l1) if i == depth - 1 else None
        outs = mlp_block(h, norm_pre_mlp[i], w_mlp_up[i].astype(BF16), w_mlp_down[i].astype(BF16),
                         norm_post_mlp[i], out_rows=out_rows)
        h = (outs[0],)
    parts = iter(outs)
    y_prompt = next(parts).reshape(n0, l0, d) if n0 * l0 else jnp.zeros((n0, l0, d), F32)
    y_sample = next(parts).reshape(n1, l1, d) if n1 * l1 else jnp.zeros((n1, l1, d), F32)
    return (y_prompt, y_sample)
```

```python
import functools
import math

import jax
import jax.numpy as jnp
from jax import lax
from jax.experimental import pallas as pl
from jax.experimental.pallas import tpu as pltpu

F32 = jnp.float32
BF16 = jnp.bfloat16

EPS = 1e-6
ROPE_THETA = 500000.0
LANES = 128
VMEM_LIMIT_BYTES = 56 * 1024 * 1024

SSD_HEADS = 16
SSD_HEAD_DIM = 64
SSD_GROUPS = 2
SSD_STATE = 128
D_SSM = SSD_HEADS * SSD_HEAD_DIM
CONV_K = 5
CONV_DIM = D_SSM + 2 * SSD_GROUPS * SSD_STATE
MLA_HEADS = 8
QK_NOPE = 128
QK_ROPE = 64
V_HEAD = 128
Q_LORA = 512
KV_LORA = 512
DIFF_HEADS = 8
DIFF_HEAD_DIM = 128
DIFF_ROT = DIFF_HEAD_DIM // 4
SSD_CHUNK = 128
NEG_BIG = -1e30
LOG2_E = math.log2(math.e)
ONES_ROWS = 16
ATTN_UNROLL = 8


def _cparams(sem):
    return pltpu.CompilerParams(dimension_semantics=sem, vmem_limit_bytes=VMEM_LIMIT_BYTES)


def _pick(n, pref):
    t = min(n, pref)
    while n % t:
        t //= 2
    return t


def _rms(x, g):
    return x * lax.rsqrt(jnp.mean(x * x, axis=-1, keepdims=True) + EPS) * g


class Seg:
    def __init__(self, j0, j1, kind, out, scale=1.0):
        self.j0, self.j1, self.kind, self.out, self.scale = j0, j1, kind, out, scale


def _row_sources(parts, tm, grid_rank):
    specs, starts = [], []
    start = 0
    for a in parts:
        nb = a.shape[0] // tm
        if grid_rank == 1:
            imap = lambda i, start=start, nb=nb: (jnp.clip(i - start, 0, nb - 1), 0)
        else:
            imap = lambda i, j, start=start, nb=nb: (jnp.clip(i - start, 0, nb - 1), 0)
        specs.append(pl.BlockSpec((tm, a.shape[1]), imap))
        starts.append(start)
        start += nb
    return specs, tuple(starts)


def _for_row_source(refs, starts, fn, extra_pred=None):
    i = pl.program_id(0)
    ends = list(starts[1:]) + [pl.num_programs(0)]
    for ref, start, end in zip(refs, starts, ends):
        pred = (i >= start) & (i < end)
        if extra_pred is not None:
            pred = pred & extra_pred
        pl.when(pred)(functools.partial(fn, ref))


def _proj_kernel(*refs, segs, n_out, has_wt, has_rope, starts):
    refs = list(refs)
    x_refs = refs[:len(starts)]
    refs = refs[len(starts) - 1:]
    g_ref, w_ref = refs[1:3]
    pos = 3
    wt_ref = None
    if has_wt:
        wt_ref = refs[pos]
        pos += 1
    if has_rope:
        c_ref, s_ref = refs[pos:pos + 2]
        pos += 2
    out_refs = refs[pos:pos + n_out]
    xn_ref = refs[pos + n_out]
    j = pl.program_id(1)

    def normalise(x_ref):
        xn_ref[...] = _rms(x_ref[...], g_ref[...]).astype(BF16)

    _for_row_source(x_refs, starts, normalise, extra_pred=j == 0)

    for seg in segs:
        o_ref = out_refs[seg.out]

        @pl.when((j >= seg.j0) & (j < seg.j1))
        def _(seg=seg, o_ref=o_ref):
            if seg.kind == 'trans':
                o_ref[...] = lax.dot_general(wt_ref[...], xn_ref[...], (((1,), (1,)), ((), ())),
                                             preferred_element_type=F32).astype(o_ref.dtype)
                return
            acc = jnp.dot(xn_ref[...], w_ref[...], preferred_element_type=F32)
            if seg.kind == 'plain':
                o_ref[...] = acc.astype(o_ref.dtype)
            else:
                c, s = c_ref[...] * seg.scale, s_ref[...] * seg.scale
                for gi in range(acc.shape[1] // LANES):
                    sl = slice(gi * LANES, (gi + 1) * LANES)
                    o_ref[:, sl] = _rope_group(acc[:, sl], c, s).astype(o_ref.dtype)


def norm_proj(h, g, w, segs, outs, tn, layout, wt=None, tabs=None, tm=1024):
    t, d = sum(a.shape[0] for a in h), h[0].shape[1]
    tm = _pick(min(layout.l0, layout.l1, tm), tm)
    n_tiles = segs[-1].j1
    has_wt, has_rope = wt is not None, tabs is not None
    w_tiles = w.shape[1] // tn
    x_specs, starts = _row_sources(h, tm, 2)
    in_specs = x_specs + [pl.BlockSpec((1, d), lambda i, j: (0, 0)),
                          pl.BlockSpec((d, tn), lambda i, j: (0, jnp.minimum(j, w_tiles - 1)))]
    args = list(h) + [g.reshape(1, d).astype(F32), w]
    if has_wt:
        in_specs.append(pl.BlockSpec((tn, d), lambda i, j: (jnp.maximum(j - w_tiles, 0), 0)))
        args.append(wt)
    if has_rope:
        tab = pl.BlockSpec((tm, LANES), lambda i, j: (layout.table_block(i, tm), 0))
        in_specs += [tab, tab]
        args += list(tabs)
    out_specs, out_shapes = [], []
    for k, (shape, dtype) in enumerate(outs):
        mine = [s for s in segs if s.out == k]
        j0, j1 = mine[0].j0, mine[-1].j1
        if mine[0].kind == 'trans':
            out_specs.append(pl.BlockSpec((tn, tm), lambda i, j, j0=j0, j1=j1: (jnp.clip(j - j0, 0, j1 - j0 - 1), i)))
        else:
            out_specs.append(pl.BlockSpec((tm, tn), lambda i, j, j0=j0, j1=j1: (i, jnp.clip(j - j0, 0, j1 - j0 - 1))))
        out_shapes.append(jax.ShapeDtypeStruct(shape, dtype))
    return pl.pallas_call(
        functools.partial(_proj_kernel, segs=segs, n_out=len(outs), has_wt=has_wt, has_rope=has_rope,
                          starts=starts),
        grid=(t // tm, n_tiles),
        in_specs=in_specs,
        out_specs=out_specs,
        out_shape=out_shapes,
        scratch_shapes=[pltpu.VMEM((tm, d), BF16)],
        compiler_params=_cparams(("parallel", "arbitrary")),
        name="norm_proj",
    )(*args)


def _outproj_kernel(*refs, n_in, starts, transposed):
    a_refs = refs[:n_in]
    w_refs = refs[n_in:2 * n_in]
    g_ref = refs[2 * n_in]
    h_refs = refs[2 * n_in + 1:-1]
    o_ref = refs[-1]
    acc = None
    for a_ref, w_ref, is_t in zip(a_refs, w_refs, transposed):
        dims = (((0,), (0,)), ((), ())) if is_t else (((1,), (0,)), ((), ()))
        part = lax.dot_general(a_ref[...], w_ref[...], dims, preferred_element_type=F32)
        acc = part if acc is None else acc + part
    branch = _rms(acc, g_ref[...])

    def add_residual(h_ref):
        o_ref[...] = h_ref[...] + branch

    _for_row_source(h_refs, starts, add_residual)


def outproj_postnorm_residual(acts, ws, transposed, g, h, tm=512):
    t, d = sum(a.shape[0] for a in h), h[0].shape[1]
    tm = _pick(min(a.shape[0] for a in h), tm)
    n_in = len(acts)
    h_specs, starts = _row_sources(h, tm, 1)
    a_specs = [pl.BlockSpec((a.shape[0], tm), lambda i: (0, i)) if is_t
               else pl.BlockSpec((tm, a.shape[1]), lambda i: (i, 0)) for a, is_t in zip(acts, transposed)]
    in_specs = (a_specs + [pl.BlockSpec(w.shape, lambda i: (0, 0)) for w in ws]
                + [pl.BlockSpec((1, d), lambda i: (0, 0))] + h_specs)
    return pl.pallas_call(
        functools.partial(_outproj_kernel, n_in=n_in, starts=starts, transposed=tuple(transposed)),
        grid=(t // tm,),
        in_specs=in_specs,
        out_specs=pl.BlockSpec((tm, d), lambda i: (i, 0)),
        out_shape=jax.ShapeDtypeStruct((t, d), F32),
        compiler_params=_cparams(("parallel",)),
        name="outproj_postnorm_residual",
    )(*acts, *ws, g.reshape(1, d).astype(F32), *h)


def _mlp_kernel(h_ref, gpre_ref, wup_ref, wdn_ref, gpost_ref, *rest, block_starts):
    out_refs = rest[:len(block_starts)]
    xn_ref, acc_ref = rest[len(block_starts):]
    i = pl.program_id(0)
    j = pl.program_id(1)

    @pl.when(j == 0)
    def _():
        xn_ref[...] = _rms(h_ref[...], gpre_ref[...]).astype(BF16)
        acc_ref[...] = jnp.zeros_like(acc_ref)

    a = jnp.dot(xn_ref[...], wup_ref[...], preferred_element_type=F32)
    a = jnp.square(jnp.maximum(a, 0.0)).astype(BF16)
    acc_ref[...] += jnp.dot(a, wdn_ref[...], preferred_element_type=F32)

    ends = list(block_starts[1:]) + [pl.num_programs(0)]
    for o_ref, start, end in zip(out_refs, block_starts, ends):
        @pl.when((j == pl.num_programs(1) - 1) & (i >= start) & (i < end))
        def _(o_ref=o_ref):
            o_ref[...] = h_ref[...] + _rms(acc_ref[...], gpost_ref[...])


def mlp_block(h, g_pre, w_up, w_down, g_post, out_rows=None, tm=512, tf=1024):
    t, d = h.shape
    f = w_up.shape[1]
    out_rows = (t,) if out_rows is None else tuple(r for r in out_rows if r)
    tm = _pick(min(out_rows), tm)
    tf = _pick(f, tf)
    block_starts, out_specs = [], []
    start = 0
    for r in out_rows:
        nb = r // tm
        out_specs.append(pl.BlockSpec((tm, d), lambda i, j, start=start, nb=nb: (jnp.clip(i - start, 0, nb - 1), 0)))
        block_starts.append(start)
        start += nb
    outs = pl.pallas_call(
        functools.partial(_mlp_kernel, block_starts=tuple(block_starts)),
        grid=(t // tm, f // tf),
        in_specs=[pl.BlockSpec((tm, d), lambda i, j: (i, 0)),
                  pl.BlockSpec((1, d), lambda i, j: (0, 0)),
                  pl.BlockSpec((d, tf), lambda i, j: (0, j)),
                  pl.BlockSpec((tf, d), lambda i, j: (j, 0)),
                  pl.BlockSpec((1, d), lambda i, j: (0, 0))],
        out_specs=out_specs,
        out_shape=[jax.ShapeDtypeStruct((r, d), F32) for r in out_rows],
        scratch_shapes=[pltpu.VMEM((tm, d), BF16), pltpu.VMEM((tm, d), F32)],
        compiler_params=_cparams(("arbitrary", "arbitrary")),
        name="mlp_block",
    )(h, g_pre.reshape(1, d).astype(F32), w_up, w_down, g_post.reshape(1, d).astype(F32))
    return outs


class SeqLayout:
    def __init__(self, n0, l0, n1, l1):
        self.n0, self.l0, self.n1, self.l1 = n0, l0, n1, l1
        self.r0 = n0 * l0
        self.t = self.r0 + n1 * l1
        assert self.r0 % l1 == 0, "sample sequences must start on a multiple of their length"

    def pos_and_len(self, row):
        in0 = row < self.r0
        pos = jnp.where(in0, lax.rem(row, self.l0), lax.rem(row - self.r0, self.l1))
        return pos, jnp.where(in0, self.l0, self.l1)

    def table_block(self, i, tm):
        b0 = self.r0 // tm
        return jnp.where(i < b0, lax.rem(i, self.l0 // tm), lax.rem(i - b0, self.l1 // tm))


HALF_LANES = LANES // 2


def rope_tables(half, n_pos):
    inv_freq = ROPE_THETA ** (-jnp.arange(half, dtype=F32) / half)
    ang = jnp.arange(n_pos, dtype=F32)[:, None] * inv_freq[None, :]
    cos, sin = jnp.cos(ang), jnp.sin(ang)
    one = jnp.ones((n_pos, HALF_LANES - half), F32)
    zero = jnp.zeros((n_pos, HALF_LANES - half), F32)
    c = jnp.concatenate([cos, one, cos, one], axis=1)
    s = jnp.concatenate([-sin, zero, sin, zero], axis=1)
    return c, s


def rope_lane_order(rot):
    half = rot // 2
    return (list(range(half)) + list(range(rot, rot + HALF_LANES - half))
            + list(range(half, rot)) + list(range(rot + HALF_LANES - half, LANES)))


def _rope_group(x, c, s):
    return x * c + pltpu.roll(x, HALF_LANES, axis=1) * s


CONV_HALO = 16


def _conv_kernel(main_ref, prev_ref, next_ref, w_ref, b_ref, o_ref, buf_ref, *, layout, tq):
    i = pl.program_id(0)
    pos, seqlen = layout.pos_and_len(i * tq)
    first = pos == 0
    last = pos + tq == seqlen
    buf_ref[0:CONV_HALO, :] = jnp.where(first, 0.0, prev_ref[...].astype(F32))
    buf_ref[CONV_HALO:CONV_HALO + tq, :] = main_ref[...].astype(F32)
    buf_ref[CONV_HALO + tq:, :] = jnp.where(last, 0.0, next_ref[...].astype(F32))
    acc = jnp.zeros(o_ref.shape, F32) + b_ref[...]
    for k in range(CONV_K):
        start = CONV_HALO + k - CONV_K // 2
        acc = acc + buf_ref[start:start + tq, :] * w_ref[k:k + 1, :]
    o_ref[...] = (acc * (1.0 / (1.0 + jnp.exp(-acc)))).astype(o_ref.dtype)


def conv_silu(zx, col0, conv_w, conv_b, layout, tq=512, tc=512):
    t = zx.shape[0]
    c = conv_w.shape[1]
    tq = _pick(min(layout.l0, layout.l1), tq)
    cb0 = col0 // tc
    hb = tq // CONV_HALO
    nh = t // CONV_HALO
    return pl.pallas_call(
        functools.partial(_conv_kernel, layout=layout, tq=tq),
        grid=(t // tq, c // tc),
        in_specs=[pl.BlockSpec((tq, tc), lambda i, j: (i, cb0 + j)),
                  pl.BlockSpec((CONV_HALO, tc), lambda i, j: (jnp.maximum(i * hb - 1, 0), cb0 + j)),
                  pl.BlockSpec((CONV_HALO, tc), lambda i, j: (jnp.minimum((i + 1) * hb, nh - 1), cb0 + j)),
                  pl.BlockSpec((CONV_K, tc), lambda i, j: (0, j)),
                  pl.BlockSpec((1, tc), lambda i, j: (0, j))],
        out_specs=pl.BlockSpec((tq, tc), lambda i, j: (i, j)),
        out_shape=jax.ShapeDtypeStruct((t, c), BF16),
        scratch_shapes=[pltpu.VMEM((tq + 2 * CONV_HALO, tc), F32)],
        compiler_params=_cparams(("parallel", "parallel")),
        name="conv_silu",
    )(zx, zx, zx, conv_w.astype(F32), conv_b.reshape(1, c).astype(F32))


def _split3(x):
    hi = x.astype(BF16)
    r1 = x - hi.astype(F32)
    mid = r1.astype(BF16)
    lo = (r1 - mid.astype(F32)).astype(BF16)
    return hi, mid, lo


def _expand_heads(v, e_ref):
    hi, mid, lo = _split3(v)
    e = e_ref[...]
    return (jnp.dot(hi, e, preferred_element_type=F32) + jnp.dot(mid, e, preferred_element_type=F32)
            + jnp.dot(lo, e, preferred_element_type=F32))


def _softplus(x):
    return jnp.maximum(x, 0.0) + jnp.log1p(jnp.exp(-jnp.abs(x)))


def _ssd_kernel(x_ref, b_ref, c_ref, dt_ref, dtt_ref, bias_ref, biast_ref, alog_ref, alogt_ref,
                e_ref, *rest, layout, rev, combine):
    if combine:
        yo_ref, z_ref, dskip_ref, gn_ref, o_ref, s_ref, y_ref = rest
    else:
        y_ref, s_ref = rest
    q = SSD_CHUNK
    hpg = SSD_HEADS // SSD_GROUPS
    i = pl.program_id(0)
    nchunks = pl.num_programs(0)
    chunk = (nchunks - 1 - i) if rev else i
    pos, seqlen = layout.pos_and_len(chunk * q)
    starts = (pos + q == seqlen) if rev else (pos == 0)

    @pl.when(starts)
    def _():
        s_ref[...] = jnp.zeros_like(s_ref)

    h0 = SSD_HEADS if rev else 0
    dt = _softplus(dt_ref[...] + bias_ref[...])
    da = dt * (-jnp.exp(alog_ref[...]))
    dt_t = _softplus(dtt_ref[...] + biast_ref[...])
    da_t = dt_t * (-jnp.exp(alogt_ref[...]))

    row = lax.broadcasted_iota(jnp.int32, (q, q), 0)
    col = lax.broadcasted_iota(jnp.int32, (q, q), 1)
    mask = (col >= row) if rev else (col <= row)
    tri = jnp.where(mask, 1.0, 0.0).astype(BF16)
    tri_t = jnp.where((row >= col) if rev else (row <= col), 1.0, 0.0).astype(BF16)

    hi, mid, lo = _split3(da)
    cum = (jnp.dot(tri, hi, preferred_element_type=F32) + jnp.dot(tri, mid, preferred_element_type=F32)
           + jnp.dot(tri, lo, preferred_element_type=F32))
    hi, mid, lo = _split3(da_t)
    cum_t = (jnp.dot(hi, tri_t, preferred_element_type=F32) + jnp.dot(mid, tri_t, preferred_element_type=F32)
             + jnp.dot(lo, tri_t, preferred_element_type=F32))

    end = 0 if rev else q - 1
    cum_end = cum[end:end + 1, :]
    ecum = jnp.exp(cum)
    wgt = dt * jnp.exp(cum_end - cum)

    xb = x_ref[...]
    bm = b_ref[...]
    cm = c_ref[...]
    s_in = s_ref[...].astype(BF16)

    for g in range(SSD_GROUPS):
        bg = bm[:, g * SSD_STATE:(g + 1) * SSD_STATE]
        cg = cm[:, g * SSD_STATE:(g + 1) * SSD_STATE]
        cb = lax.dot_general(cg, bg, (((1,), (1,)), ((), ())), preferred_element_type=F32)
        cg32 = cg.astype(F32)
        for pair in range(hpg // 2):
            lane0 = (g * hpg + 2 * pair) * SSD_HEAD_DIM
            rhs = jnp.concatenate([xb[:, lane0:lane0 + LANES], s_in[:, lane0:lane0 + LANES]], axis=0)
            res = []
            for sub in range(2):
                h = h0 + g * hpg + 2 * pair + sub
                diff = cum[:, h:h + 1] - cum_t[h:h + 1, :]
                dec = jnp.exp(jnp.where(mask, diff, NEG_BIG))
                m = (cb * dec * dt_t[h:h + 1, :]).astype(BF16)
                ce = (cg32 * ecum[:, h:h + 1]).astype(BF16)
                lhs = jnp.concatenate([m, ce], axis=1)
                res.append(jnp.dot(lhs, rhs, preferred_element_type=F32))
            lane = lax.broadcasted_iota(jnp.int32, (q, LANES), 1)
            y_ref[:, lane0:lane0 + LANES] = jnp.where(lane < SSD_HEAD_DIM, res[0], res[1])

    xw = (xb.astype(F32) * _expand_heads(wgt, e_ref)).astype(BF16)
    decay8 = _expand_heads(jnp.exp(cum[end - end % 8:end - end % 8 + 8, :]), e_ref)
    decay = decay8[end % 8:end % 8 + 1, :]
    gw = SSD_HEADS // SSD_GROUPS * SSD_HEAD_DIM
    for g in range(SSD_GROUPS):
        bg_t = bm[:, g * SSD_STATE:(g + 1) * SSD_STATE].astype(F32).T.astype(BF16)
        upd = jnp.dot(bg_t, xw[:, g * gw:(g + 1) * gw], preferred_element_type=F32)
        s_ref[:, g * gw:(g + 1) * gw] = s_ref[:, g * gw:(g + 1) * gw] * decay[:, g * gw:(g + 1) * gw] + upd

    if combine:
        z = z_ref[...].astype(F32)
        y = (yo_ref[...] + y_ref[...] + xb.astype(F32) * dskip_ref[...]) * (z * (1.0 / (1.0 + jnp.exp(-z))))
        for g in range(SSD_GROUPS):
            sl = slice(g * gw, (g + 1) * gw)
            o_ref[:, sl] = _rms(y[:, sl], gn_ref[:, sl]).astype(o_ref.dtype)


def ssd_scan(xbc, small, dt_col0, dt_t, dt_bias, a_log, layout, rev, combine=None):
    t = xbc.shape[0]
    q = SSD_CHUNK
    n = t // q
    ci = (lambda i: n - 1 - i) if rev else (lambda i: i)
    bc_w = SSD_GROUPS * SSD_STATE
    h0 = SSD_HEADS if rev else 0
    expand = jnp.zeros((LANES, D_SSM), BF16).at[h0:h0 + SSD_HEADS].set(
        jnp.repeat(jnp.eye(SSD_HEADS, dtype=BF16), SSD_HEAD_DIM, axis=1))
    full = lambda a: pl.BlockSpec(a.shape, lambda i: (0,) * a.ndim)
    pad = jnp.zeros((LANES - 2 * SSD_HEADS,), F32)
    bias = jnp.concatenate([dt_bias.astype(F32).reshape(-1), pad]).reshape(1, LANES)
    alog = jnp.concatenate([a_log.astype(F32).reshape(-1), pad]).reshape(1, LANES)
    bias_t = dt_bias.astype(F32).reshape(2 * SSD_HEADS, 1)
    alog_t = a_log.astype(F32).reshape(2 * SSD_HEADS, 1)
    args = [xbc, xbc, xbc, small, dt_t, bias, bias_t, alog, alog_t, expand]
    row = pl.BlockSpec((q, D_SSM), lambda i: (ci(i), 0))
    in_specs = [row,
                pl.BlockSpec((q, bc_w), lambda i: (ci(i), D_SSM // bc_w)),
                pl.BlockSpec((q, bc_w), lambda i: (ci(i), D_SSM // bc_w + 1)),
                pl.BlockSpec((q, LANES), lambda i: (ci(i), dt_col0 // LANES)),
                pl.BlockSpec((2 * SSD_HEADS, q), lambda i: (0, ci(i))),
                full(bias), full(bias_t), full(alog), full(alog_t), full(expand)]
    scratch = [pltpu.VMEM((SSD_STATE, D_SSM), F32)]
    if combine is not None:
        y_other, zx, d_skip, norm_w = combine
        dvec = jnp.repeat(d_skip.astype(F32), SSD_HEAD_DIM).reshape(1, D_SSM)
        gvec = norm_w.reshape(1, D_SSM).astype(F32)
        args += [y_other, zx, dvec, gvec]
        in_specs += [row, row, full(dvec), full(gvec)]
        scratch.append(pltpu.VMEM((q, D_SSM), F32))
    return pl.pallas_call(
        functools.partial(_ssd_kernel, layout=layout, rev=rev, combine=combine is not None),
        grid=(n,),
        in_specs=in_specs,
        out_specs=row,
        out_shape=jax.ShapeDtypeStruct((t, D_SSM), F32 if combine is None else BF16),
        scratch_shapes=scratch,
        compiler_params=_cparams(("arbitrary",)),
        name="ssd_scan_bwd" if rev else "ssd_scan_fwd",
    )(*args)


def _mla_prep_kernel(cq_ref, ckv_ref, kpe_ref, gq_ref, gkv_ref, wq_ref, wk_ref, wvt_ref, c_ref, s_ref,
                     q_ref, k_ref, vt_ref, *, scale):
    c, s = c_ref[...], s_ref[...]
    nw = MLA_HEADS * QK_NOPE
    qn = _rms(cq_ref[...], gq_ref[...]).astype(BF16)
    q = jnp.dot(qn, wq_ref[...], preferred_element_type=F32) * scale
    kvn = _rms(ckv_ref[...], gkv_ref[...]).astype(BF16)
    kn = jnp.dot(kvn, wk_ref[...], preferred_element_type=F32)
    kpe = _rope_group(kpe_ref[...], c, s).astype(BF16)
    for h in range(MLA_HEADS):
        lo = h * 2 * LANES
        q_ref[:, lo:lo + LANES] = q[:, h * LANES:(h + 1) * LANES].astype(BF16)
        qpe = q[:, nw + h * LANES:nw + (h + 1) * LANES]
        q_ref[:, lo + LANES:lo + 2 * LANES] = _rope_group(qpe, c, s).astype(BF16)
        k_ref[:, lo:lo + LANES] = kn[:, h * LANES:(h + 1) * LANES].astype(BF16)
        k_ref[:, lo + LANES:lo + 2 * LANES] = kpe
    vt_ref[...] = lax.dot_general(wvt_ref[...], kvn, (((1,), (1,)), ((), ())),
                                  preferred_element_type=F32).astype(BF16)


def mla_prep(small, gq, gkv, wq, wk, wvt, tabs, layout, tm=512):
    t = small.shape[0]
    tm = _pick(min(layout.l0, layout.l1), tm)
    hq = MLA_HEADS * 2 * LANES
    tab = pl.BlockSpec((tm, LANES), lambda i: (layout.table_block(i, tm), 0))
    full = lambda a: pl.BlockSpec(a.shape, lambda i: (0,) * a.ndim)
    gq2 = gq.reshape(1, Q_LORA).astype(F32)
    gkv2 = gkv.reshape(1, KV_LORA).astype(F32)
    return pl.pallas_call(
        functools.partial(_mla_prep_kernel, scale=LOG2_E / math.sqrt(QK_NOPE + QK_ROPE)),
        grid=(t // tm,),
        in_specs=[pl.BlockSpec((tm, Q_LORA), lambda i: (i, 0)),
                  pl.BlockSpec((tm, KV_LORA), lambda i: (i, 1)),
                  pl.BlockSpec((tm, LANES), lambda i: (i, (Q_LORA + KV_LORA) // LANES)),
                  full(gq2), full(gkv2), full(wq), full(wk), full(wvt), tab, tab],
        out_specs=[pl.BlockSpec((tm, hq), lambda i: (i, 0)),
                   pl.BlockSpec((tm, hq), lambda i: (i, 0)),
                   pl.BlockSpec((MLA_HEADS * V_HEAD, tm), lambda i: (0, i))],
        out_shape=[jax.ShapeDtypeStruct((t, hq), BF16),
                   jax.ShapeDtypeStruct((t, hq), BF16),
                   jax.ShapeDtypeStruct((MLA_HEADS * V_HEAD, t), BF16)],
        compiler_params=_cparams(("parallel",)),
        name="mla_prep",
    )(small, small, small, gq2, gkv2, wq, wk, wvt, *tabs)


def _reduce_rows(x, op):
    rows = x.shape[0]
    while rows > 8 and rows % 8 == 0:
        x = op(x.reshape(8, rows // 8, x.shape[1]), axis=0)
        rows //= 8
    return op(x, axis=0, keepdims=True)


def _flash_kernel(*refs, n_maps, seq_len, bk, lam_init, has_prev):
    if has_prev:
        n_in = 6 if n_maps == 2 else 4
        refs = refs[:n_in] + refs[n_in + 1:]
    if n_maps == 2:
        q_ref, qn_ref, k_ref, vt_ref, lam_ref, subln_ref, o_ref, m_ref, l_ref, acc_ref, sa_ref, sb_ref = refs
    else:
        q_ref, qn_ref, k_ref, vt_ref, o_ref, m_ref, l_ref, acc_ref, sa_ref, sb_ref = refs
    n_streams = m_ref.shape[0]
    dqk = q_ref.shape[1] // n_maps
    rows = q_ref.shape[0] * n_maps // n_streams
    col = lambda t: slice(t * dqk, (t + 1) * dqk) if n_maps == 2 else slice(None)
    row = lambda t: slice(None) if n_maps == 2 else slice(t * rows, (t + 1) * rows)
    dv = vt_ref.shape[0]
    n_chunks = seq_len // bk
    chained = n_chunks % 2 == 0
    m_ref[...] = jnp.full_like(m_ref, NEG_BIG)
    l_ref[...] = jnp.zeros_like(l_ref)
    acc_ref[...] = jnp.zeros_like(acc_ref)

    def scores(j, t, query_ref=q_ref):
        off = pl.multiple_of(j * bk, bk)
        return lax.dot_general(k_ref[pl.ds(off, bk), col(t)], query_ref[row(t), col(t)],
                               (((1,), (1,)), ((), ())), preferred_element_type=F32)

    def accumulate(j, t, s):
        off = pl.multiple_of(j * bk, bk)
        m_old = m_ref[t]
        m_new = jnp.maximum(m_old, _reduce_rows(s, jnp.max))
        alpha = jnp.exp2(m_old - m_new)
        p = jnp.exp2(s - m_new).astype(BF16)
        m_ref[t] = m_new
        vt1 = jnp.concatenate([vt_ref[:, pl.ds(off, bk)], jnp.ones((ONES_ROWS, bk), BF16)], axis=0)
        pv = jnp.dot(vt1, p, preferred_element_type=F32)
        l_ref[t] = alpha * l_ref[t] + pv[dv:dv + 1]
        acc_ref[t] = alpha * acc_ref[t] + pv[:dv]

    def step(j, cur_ref, next_ref, has_next):
        for t in range(n_streams):
            if has_next:
                next_ref[t] = scores(j + 1, t)
            elif chained:
                next_ref[t] = scores(0, t, qn_ref)
            accumulate(j, t, cur_ref[t])

    def first_scores():
        for t in range(n_streams):
            sa_ref[t] = scores(0, t)

    if chained:
        pl.when(pl.program_id(2) == 0)(first_scores)
    else:
        first_scores()
    bufs = (sa_ref, sb_ref)
    n_loop = (n_chunks - 1) // ATTN_UNROLL

    def body(i, carry):
        for u in range(ATTN_UNROLL):
            step(ATTN_UNROLL * i + u, bufs[u % 2], bufs[1 - u % 2], True)
        return carry

    lax.fori_loop(0, n_loop, body, 0)
    for j in range(ATTN_UNROLL * n_loop, n_chunks):
        step(j, bufs[j % 2], bufs[1 - j % 2], j + 1 < n_chunks)

    if n_maps == 1:
        o_ref[...] = (acc_ref[0] * (1.0 / l_ref[0])).astype(o_ref.dtype)
    else:
        lp = lam_ref[...]
        lam = (jnp.exp(jnp.sum(lp[0:1] * lp[1:2], axis=-1, keepdims=True))
               - jnp.exp(jnp.sum(lp[2:3] * lp[3:4], axis=-1, keepdims=True)) + lam_init)
        o = acc_ref[0] * (1.0 / l_ref[0]) - acc_ref[1] * (lam / l_ref[1])
        inv = lax.rsqrt(jnp.mean(o * o, axis=0, keepdims=True) + EPS) * (1.0 - lam_init)
        gain = jnp.tile(subln_ref[...], (1, o.shape[1] // LANES))
        o_ref[...] = (o * inv * gain).astype(o_ref.dtype)


def flash_attention(q, k, vt, n_heads, n_maps, q_col0, k_col0, v_row0, row0, n_seq, seq_len,
                    lam=None, subln=None, lam_init=0.0, prev=None, bq=512, bk=512):
    hq = 2 * LANES
    dv = LANES * n_maps
    bq = _pick(seq_len, bq)
    bk = _pick(seq_len, bk)
    nq = seq_len // bq
    ns = n_maps
    bqs = bq * n_maps // ns
    assert row0 % seq_len == 0 and q_col0 % hq == 0 and k_col0 % hq == 0 and v_row0 % dv == 0
    qb0, sb0 = row0 // bq, row0 // seq_len
    in_specs = [pl.BlockSpec((bq, hq), lambda b, h, i: (qb0 + b * nq + i, q_col0 // hq + h)),
                pl.BlockSpec((bq, hq), lambda b, h, i: (qb0 + b * nq + jnp.minimum(i + 1, nq - 1), q_col0 // hq + h)),
                pl.BlockSpec((seq_len, hq), lambda b, h, i: (sb0 + b, k_col0 // hq + h)),
                pl.BlockSpec((dv, seq_len), lambda b, h, i: (v_row0 // dv + h, sb0 + b))]
    args = [q, q, k, vt]
    if n_maps == 2:
        in_specs += [pl.BlockSpec(lam.shape, lambda b, h, i: (0, 0)),
                     pl.BlockSpec(subln.shape, lambda b, h, i: (0, 0))]
        args += [lam, subln]
    aliases = {}
    if prev is not None:
        in_specs.append(pl.BlockSpec(memory_space=pl.ANY))
        aliases = {len(args): 0}
        args.append(prev)
    return pl.pallas_call(
        functools.partial(_flash_kernel, n_maps=n_maps, seq_len=seq_len, bk=bk, lam_init=lam_init,
                          has_prev=prev is not None),
        grid=(n_seq, n_heads, nq),
        in_specs=in_specs,
        out_specs=pl.BlockSpec((dv, bq), lambda b, h, i: (h, qb0 + b * nq + i)),
        out_shape=jax.ShapeDtypeStruct((n_heads * dv, q.shape[0]), BF16),
        scratch_shapes=[pltpu.VMEM((ns, 1, bqs), F32), pltpu.VMEM((ns, 1, bqs), F32),
                        pltpu.VMEM((ns, dv, bqs), F32),
                        pltpu.VMEM((ns, bk, bqs), F32), pltpu.VMEM((ns, bk, bqs), F32)],
        input_output_aliases=aliases,
        compiler_params=_cparams(("parallel", "parallel", "arbitrary")),
        name="diff_attention" if n_maps == 2 else "mla_attention",
    )(*args)


def attention_all(layout, **kw):
    out = None
    if layout.n0:
        out = flash_attention(row0=0, n_seq=layout.n0, seq_len=layout.l0, prev=out, **kw)
    if layout.n1:
        out = flash_attention(row0=layout.r0, n_seq=layout.n1, seq_len=layout.l1, prev=out, **kw)
    return out


def _even_in_weight(w):
    o1 = D_SSM
    o2 = o1 + CONV_DIM
    o3 = o2 + 2 * SSD_HEADS
    o4 = o3 + Q_LORA
    o5 = o4 + KV_LORA
    d = w.shape[0]
    zeros = lambda n: jnp.zeros((d, n), w.dtype)
    k_pe = jnp.concatenate([w[:, o5:], zeros(LANES - QK_ROPE)], axis=1)[:, jnp.array(rope_lane_order(QK_ROPE))]
    return jnp.concatenate([w[:, :o2], w[:, o3:o4], w[:, o4:o5], k_pe,
                            w[:, o2:o3], zeros(LANES - 2 * SSD_HEADS), zeros(EVEN_SMALL_PAD)], axis=1).astype(BF16)


def _mla_up_weights(w_q_up, w_kv_up):
    wq = w_q_up.reshape(Q_LORA, MLA_HEADS, QK_NOPE + QK_ROPE)
    q_nope = wq[:, :, :QK_NOPE].reshape(Q_LORA, MLA_HEADS * QK_NOPE)
    q_pe = jnp.pad(wq[:, :, QK_NOPE:], ((0, 0), (0, 0), (0, LANES - QK_ROPE)))
    q_pe = q_pe[:, :, jnp.array(rope_lane_order(QK_ROPE))].reshape(Q_LORA, MLA_HEADS * LANES)
    wkv = w_kv_up.reshape(KV_LORA, MLA_HEADS, QK_NOPE + V_HEAD)
    k_nope = wkv[:, :, :QK_NOPE].reshape(KV_LORA, MLA_HEADS * QK_NOPE)
    v = wkv[:, :, QK_NOPE:].reshape(KV_LORA, MLA_HEADS * V_HEAD)
    return (jnp.concatenate([q_nope, q_pe], axis=1).astype(BF16), k_nope.astype(BF16), v.T.astype(BF16))


EVEN_TN = 512
EVEN_SMALL = Q_LORA + KV_LORA + 2 * LANES
EVEN_SMALL_PAD = -EVEN_SMALL % EVEN_TN


def _even_mixer(h, g, p, i, layout, tabs):
    t = layout.t
    w_in = _even_in_weight(p['w_in_even'][i])
    n_main, n_small = D_SSM + CONV_DIM, EVEN_SMALL + EVEN_SMALL_PAD
    jm = n_main // EVEN_TN
    segs = [Seg(0, jm, 'plain', 0), Seg(jm, jm + n_small // EVEN_TN, 'plain', 1)]
    zx, small = norm_proj(h, g, w_in, segs, [((t, n_main), BF16), ((t, n_small), F32)], EVEN_TN, layout)
    dt_col0 = Q_LORA + KV_LORA + LANES

    xbc = conv_silu(zx, D_SSM, p['conv_w'][i], p['conv_b'][i], layout)
    dt_t = small[:, dt_col0:dt_col0 + 2 * SSD_HEADS].T
    y_f = ssd_scan(xbc, small, dt_col0, dt_t, p['dt_bias'][i], p['a_log'][i], layout, rev=False)
    y = ssd_scan(xbc, small, dt_col0, dt_t, p['dt_bias'][i], p['a_log'][i], layout, rev=True,
                 combine=(y_f, zx, p['d_skip'][i], p['ssm_norm_w'][i]))

    wq, wk, wvt = _mla_up_weights(p['w_q_up'][i], p['w_kv_up'][i])
    q, k, vt = mla_prep(small, p['q_norm_w'][i], p['kv_norm_w'][i], wq, wk, wvt, tabs, layout)
    o = attention_all(layout, q=q, k=k, vt=vt, n_heads=MLA_HEADS, n_maps=1, q_col0=0, k_col0=0, v_row0=0)
    w_out = p['w_out_even'][i].astype(BF16)
    return (y, o), (w_out[:D_SSM], w_out[D_SSM:]), (False, True)


DIFF_TN = 512


def _diff_mixer(h, g, p, j, layer_idx, layout, tabs):
    t = layout.t
    d_diff = DIFF_HEADS * 2 * DIFF_HEAD_DIM
    w_in = p['w_in_odd'][j].astype(BF16)
    order = jnp.array(rope_lane_order(DIFF_ROT))
    w_qk = w_in[:, :2 * d_diff].reshape(-1, 2 * d_diff // LANES, LANES)[:, :, order].reshape(-1, 2 * d_diff)
    jq = d_diff // DIFF_TN
    segs = [Seg(0, jq, 'rope', 0, scale=LOG2_E / math.sqrt(DIFF_HEAD_DIM)),
            Seg(jq, 2 * jq, 'rope', 0),
            Seg(2 * jq, 3 * jq, 'trans', 1)]
    qk, vt = norm_proj(h, g, w_qk, segs, [((t, 2 * d_diff), BF16), ((d_diff, t), BF16)],
                       DIFF_TN, layout, wt=w_in[:, 2 * d_diff:].T, tabs=tabs)
    lam_init = 0.8 - 0.6 * math.exp(-0.3 * layer_idx)
    o = attention_all(layout, q=qk, k=qk, vt=vt, n_heads=DIFF_HEADS, n_maps=2, q_col0=0, k_col0=d_diff,
                      v_row0=0, lam=p['diff_lambda'][j].astype(F32),
                      subln=jnp.broadcast_to(p['subln_w'][j].astype(F32)[:, None], (2 * DIFF_HEAD_DIM, LANES)),
                      lam_init=lam_init)
    return (o,), (p['w_out_odd'][j].astype(BF16),), (True,)


def kernel(x_prompt, x_sample, norm_pre_mix, norm_post_mix, norm_pre_mlp, norm_post_mlp,
           w_in_even, conv_w, conv_b, dt_bias, a_log, d_skip, ssm_norm_w, q_norm_w, w_q_up,
           kv_norm_w, w_kv_up, w_out_even, w_in_odd, diff_lambda, subln_w, w_out_odd,
           w_mlp_up, w_mlp_down):
    p = dict(w_in_even=w_in_even, conv_w=conv_w, conv_b=conv_b, dt_bias=dt_bias, a_log=a_log,
             d_skip=d_skip, ssm_norm_w=ssm_norm_w, q_norm_w=q_norm_w, w_q_up=w_q_up,
             kv_norm_w=kv_norm_w, w_kv_up=w_kv_up, w_out_even=w_out_even, w_in_odd=w_in_odd,
             diff_lambda=diff_lambda, subln_w=subln_w, w_out_odd=w_out_odd)
    n0, l0, d = x_prompt.shape
    n1, l1, _ = x_sample.shape
    layout = SeqLayout(n0, l0, n1, l1)
    depth = norm_pre_mix.shape[0]
    n_pos = max(l0, l1)
    tabs_mla = rope_tables(QK_ROPE // 2, n_pos)
    tabs_diff = rope_tables(DIFF_ROT // 2, n_pos)

    h = tuple(a for a in (x_prompt.reshape(n0 * l0, d), x_sample.reshape(n1 * l1, d)) if a.shape[0])
    for i in range(depth):
        if i % 2 == 0:
            acts, ws, transposed = _even_mixer(h, norm_pre_mix[i], p, i // 2, layout, tabs_mla)
        else:
            acts, ws, transposed = _diff_mixer(h, norm_pre_mix[i], p, i // 2, i, layout, tabs_diff)
        h = outproj_postnorm_residual(acts, ws, transposed, norm_post_mix[i], h)
        out_rows = (n0 * l0, n1 * l1) if i == depth - 1 else None
        outs = mlp_block(h, norm_pre_mlp[i], w_mlp_up[i].astype(BF16), w_mlp_down[i].astype(BF16),
                         norm_post_mlp[i], out_rows=out_rows)
        h = (outs[0],)
    parts = iter(outs)
    y_prompt = next(parts).reshape(n0, l0, d) if n0 * l0 else jnp.zeros((n0, l0, d), F32)
    y_sample = next(parts).reshape(n1, l1, d) if n1 * l1 else jnp.zeros((n1, l1, d), F32)
    return (y_prompt, y_sample)
```
